```python
import math
import jax, jax.numpy as jnp
from jax import lax
import numpy as np

D_MODEL = 2048
BATCH = 8
SEQ = 2048
DEPTH = 2
DEC_BATCH = 128
DEC_SEQ = 4
PAST_LEN = 16384
PAGE_SIZE = 128

D_MIX = D_MODEL
D_POOL = D_MIX // 4
POOL_GROUPS = 4
POOL_GW = D_POOL // POOL_GROUPS
POOL_WINDOWS = (2, 4, 8, 16)
POOL_STATE = max(POOL_WINDOWS) - 1
D_RNN = D_MIX // 4
RNN_BLOCKS = 8
RNN_BW = D_RNN // RNN_BLOCKS
CONV_W = 4
LRU_C = 8.0
MLA_HEADS = 8
D_NOPE = 128
D_ROPE = 64
D_V = 128
R_Q = D_MODEL // 4
R_KV = MLA_HEADS * D_V // 4
ROPE_BASE = 10000.0
Q_BLOCK = 128
MEM_LEN = 256
MEM_HEADS = 4
MEM_HD = 128
D_FF = 5632
N_EXPERTS = 8
TOP_K = 2
D_EXPERT = 1408
EPS = 1e-6

OFF_POOL = 0
OFF_RX = OFF_POOL + D_POOL
OFF_RG = OFF_RX + D_RNN
OFF_CQ = OFF_RG + D_RNN
OFF_CKV = OFF_CQ + R_Q
OFF_KR = OFF_CKV + R_KV
IN_COLS = OFF_KR + D_ROPE
D_CAT = D_POOL + D_RNN + MLA_HEADS * D_V
N_DENSE = (DEPTH + 1) // 2
N_MOE = DEPTH // 2

kernel_name = "hybrid_pool_rglru_mla_decoder_step"


def rmsnorm(x, g):
    x32 = x.astype(jnp.float32)
    y = x32 * lax.rsqrt(jnp.mean(x32 * x32, axis=-1, keepdims=True) + EPS)
    return (y * g.astype(jnp.float32)).astype(x.dtype)


def rope(x, pos):
    half = x.shape[-1] // 2
    freqs = ROPE_BASE ** (-jnp.arange(half, dtype=jnp.float32) / half)
    ang = pos.astype(jnp.float32)[:, None] * freqs[None, :]
    cos = jnp.cos(ang)[None, :, None, :]
    sin = jnp.sin(ang)[None, :, None, :]
    x32 = x.astype(jnp.float32)
    x1, x2 = x32[..., :half], x32[..., half:]
    return jnp.concatenate([x1 * cos - x2 * sin, x2 * cos + x1 * sin], axis=-1).astype(x.dtype)


def pool_mixer(u, prefix, pos, w_pool, pool_scale):
    L = u.shape[1]
    ext = jnp.concatenate([prefix, u], axis=1)
    s = jnp.cumsum(ext.astype(jnp.float32), axis=1)
    s = jnp.pad(s, ((0, 0), (1, 0), (0, 0)))
    hi = s[:, POOL_STATE + 1:POOL_STATE + 1 + L]
    u32 = u.astype(jnp.float32)
    outs = []
    for g, w in enumerate(POOL_WINDOWS):
        c = slice(g * POOL_GW, (g + 1) * POOL_GW)
        lo = s[:, POOL_STATE + 1 - w:POOL_STATE + 1 - w + L, c]
        cnt = jnp.minimum(pos + 1, w).astype(jnp.float32)[None, :, None]
        outs.append((hi[..., c] - lo) / cnt - u32[..., c])
    p = jnp.stack(outs, axis=2).astype(u.dtype)
    y = jnp.einsum('blgc,gcd->blgd', p, w_pool).reshape(u.shape)
    return y * pool_scale, ext[:, -POOL_STATE:]


def rglru_mixer(xb, gb, conv_prefix, h0, conv_w, conv_b, w_a, b_a, w_x, b_x, lam):
    B, L, C = xb.shape
    ext = jnp.concatenate([conv_prefix, xb], axis=1)
    xc = conv_b + conv_w[0] * ext[:, 0:L]
    for k in range(1, CONV_W):
        xc = xc + conv_w[k] * ext[:, k:k + L]
    xr = xc.reshape(B, L, RNN_BLOCKS, RNN_BW)
    r = jax.nn.sigmoid((jnp.einsum('blnc,ncd->blnd', xr, w_a).reshape(B, L, C) + b_a).astype(jnp.float32))
    i = jax.nn.sigmoid((jnp.einsum('blnc,ncd->blnd', xr, w_x).reshape(B, L, C) + b_x).astype(jnp.float32))
    log_a = -LRU_C * r * jax.nn.softplus(-lam.astype(jnp.float32))
    a = jnp.exp(log_a)
    b = jnp.sqrt(-jnp.expm1(2.0 * log_a)) * (i * xc.astype(jnp.float32))
    b = b.at[:, 0].add(a[:, 0] * h0.astype(jnp.float32))

    def combine(left, right):
        a1, b1 = left
        a2, b2 = right
        return a1 * a2, a2 * b1 + b2

    _, h = lax.associative_scan(combine, (a, b), axis=1)
    y = jax.nn.gelu(gb.astype(jnp.float32)) * h
    return y.astype(xb.dtype), ext[:, -(CONV_W - 1):], h[:, -1].astype(h0.dtype)


def latent_attention(q_abs, q_rope, ckv, kr, q_pos, k_pos):
    B, L, H, R = q_abs.shape
    nb = L // Q_BLOCK if L % Q_BLOCK == 0 else 1
    bs = L // nb
    scale = (D_NOPE + D_ROPE) ** -0.5

    def split(t):
        return jnp.swapaxes(t.reshape((B, nb, bs) + t.shape[2:]), 0, 1)

    def block(args):
        qa, qr, qp = args
        s = (jnp.einsum('bqhr,bkr->bhqk', qa, ckv) + jnp.einsum('bqhe,bke->bhqk', qr, kr)).astype(jnp.float32) * scale
        s = jnp.where(k_pos[None, None, None, :] <= qp[None, None, :, None], s, -jnp.inf)
        p = jax.nn.softmax(s, axis=-1).astype(ckv.dtype)
        return jnp.einsum('bhqk,bkr->bqhr', p, ckv)

    o = lax.map(block, (split(q_abs), split(q_rope), q_pos.reshape(nb, bs)))
    return jnp.swapaxes(o, 0, 1).reshape(B, L, H, R)


def mla_mixer(cq, ckv_raw, kr_raw, pos, ckv_past, kr_past, q_norm, w_uq, kv_norm, w_uk, w_uv):
    B, L, _ = cq.shape
    q = jnp.einsum('blr,rhe->blhe', rmsnorm(cq, q_norm), w_uq)
    q_nope = q[..., :D_NOPE]
    q_rope = rope(q[..., D_NOPE:], pos)
    ckv = rmsnorm(ckv_raw, kv_norm)
    kr = rope(kr_raw[:, :, None, :], pos)[:, :, 0]
    q_abs = jnp.einsum('blhn,rhn->blhr', q_nope, w_uk)
    if ckv_past is None:
        ckv_all, kr_all, k_pos = ckv, kr, pos
    else:
        n_past = ckv_past.shape[1]
        ckv_all = jnp.concatenate([ckv_past, ckv], axis=1)
        kr_all = jnp.concatenate([kr_past, kr], axis=1)
        k_pos = jnp.concatenate([jnp.arange(n_past, dtype=pos.dtype), pos])
    o_lat = latent_attention(q_abs, q_rope, ckv_all, kr_all, pos, k_pos)
    y = jnp.einsum('blhr,rhv->blhv', o_lat, w_uv).reshape(B, L, MLA_HEADS * D_V)
    return y, ckv, kr


def memory_kv(mem, g, w_mk, w_mv):
    m = rmsnorm(mem, g)
    return jnp.einsum('bmd,dhe->bmhe', m, w_mk), jnp.einsum('bmd,dhe->bmhe', m, w_mv)


def mem_attention(xn, mk, mv, w_mq, w_mo):
    q = jnp.einsum('bld,dhe->blhe', xn, w_mq)
    s = jnp.einsum('blhe,bmhe->bhlm', q, mk).astype(jnp.float32) * (MEM_HD ** -0.5)
    p = jax.nn.softmax(s, axis=-1).astype(mv.dtype)
    o = jnp.einsum('bhlm,bmhe->blhe', p, mv)
    return jnp.einsum('blhe,hed->bld', o, w_mo)


def swiglu(x, wg, wu, wd):
    return (jax.nn.silu(x @ wg) * (x @ wu)) @ wd


def moe_ffn(x, router, wg, wu, wd):
    probs = jax.nn.softmax((x @ router).astype(jnp.float32), axis=-1)
    top_p, top_i = lax.top_k(probs, TOP_K)
    top_p = top_p / jnp.sum(top_p, axis=-1, keepdims=True)
    gates = jnp.sum(jax.nn.one_hot(top_i, N_EXPERTS, dtype=jnp.float32) * top_p[..., None], axis=-2).astype(x.dtype)
    out = jnp.zeros_like(x)
    for e in range(N_EXPERTS):
        out = out + gates[..., e:e + 1] * swiglu(x, wg[e], wu[e], wd[e])
    return out


def decoder_layer(x, pos, mem_k, mem_v, pool_prefix, conv_prefix, h0, ckv_past, kr_past, lw, ffn):
    hn = rmsnorm(x, lw['norm_mix'])
    proj = hn @ lw['w_in']
    u = proj[..., OFF_POOL:OFF_RX]
    xb = proj[..., OFF_RX:OFF_RG]
    gb = proj[..., OFF_RG:OFF_CQ]
    cq = proj[..., OFF_CQ:OFF_CKV]
    ckv_raw = proj[..., OFF_CKV:OFF_KR]
    kr_raw = proj[..., OFF_KR:IN_COLS]
    y_pool, pool_new = pool_mixer(u, pool_prefix, pos, lw['w_pool'], lw['pool_scale'])
    y_rnn, conv_new, h_new = rglru_mixer(xb, gb, conv_prefix, h0, lw['conv_w'], lw['conv_b'],
                                         lw['w_rg_a'], lw['b_rg_a'], lw['w_rg_x'], lw['b_rg_x'], lw['lru_lambda'])
    y_mla, ckv, kr = mla_mixer(cq, ckv_raw, kr_raw, pos, ckv_past, kr_past, lw['q_norm'], lw['w_uq'],
                               lw['kv_norm'], lw['w_uk'], lw['w_uv'])
    x = x + jnp.concatenate([y_pool, y_rnn, y_mla], axis=-1) @ lw['w_out']
    x = x + mem_attention(rmsnorm(x, lw['norm_mem']), mem_k, mem_v, lw['w_mq'], lw['w_mo'])
    hn = rmsnorm(x, lw['norm_ffn'])
    if ffn[0] == 'dense':
        x = x + swiglu(hn, ffn[1], ffn[2], ffn[3])
    else:
        x = x + moe_ffn(hn, ffn[1], ffn[2], ffn[3], ffn[4])
    return x, (pool_new, conv_new, h_new, ckv, kr)


def setup_inputs(seed: int = 0) -> dict:
    key = jax.random.key(seed)
    ks = iter(jax.random.split(key, 64))
    f32 = jnp.float32

    def nrm(shape, scale=1.0):
        return jax.random.normal(next(ks), shape, f32) * scale

    def gain(shape):
        return 1.0 + 0.05 * jax.random.normal(next(ks), shape, f32)

    D = D_MODEL
    n_pages = PAST_LEN // PAGE_SIZE
    n_used = DEC_BATCH * n_pages
    n_pool = n_used + max(1, n_used // 4)
    page_table = jax.random.permutation(next(ks), n_pool)[:n_used].reshape(DEC_BATCH, n_pages).astype(jnp.int32)
    a0 = jax.random.uniform(next(ks), (DEPTH, D_RNN), f32, 0.9, 0.999)
    return {
        'x_prompt': nrm((BATCH, SEQ, D)),
        'x_sample': nrm((DEC_BATCH, DEC_SEQ, D)),
        'mem_prompt': nrm((BATCH, MEM_LEN, D)),
        'cache_ckv': nrm((DEPTH, n_pool, PAGE_SIZE, R_KV)),
        'cache_krope': nrm((DEPTH, n_pool, PAGE_SIZE, D_ROPE)),
        'page_table': page_table,
        'state_pool': nrm((DEPTH, DEC_BATCH, POOL_STATE, D_POOL)),
        'state_conv': nrm((DEPTH, DEC_BATCH, CONV_W - 1, D_RNN)),
        'state_h': nrm((DEPTH, DEC_BATCH, D_RNN), 0.5),
        'cache_mem_k': nrm((DEPTH, DEC_BATCH, MEM_LEN, MEM_HEADS, MEM_HD)),
        'cache_mem_v': nrm((DEPTH, DEC_BATCH, MEM_LEN, MEM_HEADS, MEM_HD)),
        'norm_mix': gain((DEPTH, D)),
        'norm_mem': gain((DEPTH, D)),
        'norm_memkv': gain((DEPTH, D)),
        'norm_ffn': gain((DEPTH, D)),
        'norm_final': gain((D,)),
        'w_in': nrm((DEPTH, D, IN_COLS), D ** -0.5),
        'w_pool': nrm((DEPTH, POOL_GROUPS, POOL_GW, POOL_GW), POOL_GW ** -0.5),
        'pool_scale': gain((DEPTH, D_POOL)),
        'conv_w': nrm((DEPTH, CONV_W, D_RNN), CONV_W ** -0.5),
        'conv_b': nrm((DEPTH, D_RNN), 0.02),
        'w_rg_a': nrm((DEPTH, RNN_BLOCKS, RNN_BW, RNN_BW), RNN_BW ** -0.5),
        'b_rg_a': nrm((DEPTH, D_RNN), 0.02),
        'w_rg_x': nrm((DEPTH, RNN_BLOCKS, RNN_BW, RNN_BW), RNN_BW ** -0.5),
        'b_rg_x': nrm((DEPTH, D_RNN), 0.02),
        'lru_lambda': jnp.log(a0) - jnp.log1p(-a0),
        'q_norm': gain((DEPTH, R_Q)),
        'w_uq': nrm((DEPTH, R_Q, MLA_HEADS, D_NOPE + D_ROPE), R_Q ** -0.5),
        'kv_norm': gain((DEPTH, R_KV)),
        'w_uk': nrm((DEPTH, R_KV, MLA_HEADS, D_NOPE), R_KV ** -0.5),
        'w_uv': nrm((DEPTH, R_KV, MLA_HEADS, D_V), R_KV ** -0.5),
        'w_out': nrm((DEPTH, D_CAT, D), D_CAT ** -0.5),
        'w_mq': nrm((DEPTH, D, MEM_HEADS, MEM_HD), D ** -0.5),
        'w_mk': nrm((DEPTH, D, MEM_HEADS, MEM_HD), D ** -0.5),
        'w_mv': nrm((DEPTH, D, MEM_HEADS, MEM_HD), D ** -0.5),
        'w_mo': nrm((DEPTH, MEM_HEADS, MEM_HD, D), (MEM_HEADS * MEM_HD) ** -0.5),
        'ffn_w_gate': nrm((N_DENSE, D, D_FF), D ** -0.5),
        'ffn_w_up': nrm((N_DENSE, D, D_FF), D ** -0.5),
        'ffn_w_down': nrm((N_DENSE, D_FF, D), D_FF ** -0.5),
        'moe_router': nrm((N_MOE, D, N_EXPERTS), D ** -0.5),
        'moe_w_gate': nrm((N_MOE, N_EXPERTS, D, D_EXPERT), D ** -0.5),
        'moe_w_up': nrm((N_MOE, N_EXPERTS, D, D_EXPERT), D ** -0.5),
        'moe_w_down': nrm((N_MOE, N_EXPERTS, D_EXPERT, D), D_EXPERT ** -0.5),
    }


def reference(x_prompt, x_sample, mem_prompt, cache_ckv, cache_krope, page_table, state_pool, state_conv,
              state_h, cache_mem_k, cache_mem_v, norm_mix, norm_mem, norm_memkv, norm_ffn, norm_final, w_in,
              w_pool, pool_scale, conv_w, conv_b, w_rg_a, b_rg_a, w_rg_x, b_rg_x, lru_lambda, q_norm, w_uq,
              kv_norm, w_uk, w_uv, w_out, w_mq, w_mk, w_mv, w_mo, ffn_w_gate, ffn_w_up, ffn_w_down,
              moe_router, moe_w_gate, moe_w_up, moe_w_down):
    bp, lp = x_prompt.shape[0], x_prompt.shape[1]
    bsm, lsm = x_sample.shape[0], x_sample.shape[1]
    pos_p = jnp.arange(lp, dtype=jnp.int32)
    pos_s = PAST_LEN + jnp.arange(lsm, dtype=jnp.int32)
    dt = x_prompt.dtype
    hp, hs = x_prompt, x_sample
    pc, pk, pp, pcv, ph, pmk, pmv = [], [], [], [], [], [], []
    sc, sk, sp, scv, sh = [], [], [], [], []
    for l in range(DEPTH):
        lw = {'norm_mix': norm_mix[l], 'norm_mem': norm_mem[l], 'norm_ffn': norm_ffn[l], 'w_in': w_in[l],
              'w_pool': w_pool[l], 'pool_scale': pool_scale[l], 'conv_w': conv_w[l], 'conv_b': conv_b[l],
              'w_rg_a': w_rg_a[l], 'b_rg_a': b_rg_a[l], 'w_rg_x': w_rg_x[l], 'b_rg_x': b_rg_x[l],
              'lru_lambda': lru_lambda[l], 'q_norm': q_norm[l], 'w_uq': w_uq[l], 'kv_norm': kv_norm[l],
              'w_uk': w_uk[l], 'w_uv': w_uv[l], 'w_out': w_out[l], 'w_mq': w_mq[l], 'w_mo': w_mo[l]}
        j = l // 2
        if l % 2 == 0:
            ffn = ('dense', ffn_w_gate[j], ffn_w_up[j], ffn_w_down[j])
        else:
            ffn = ('moe', moe_router[j], moe_w_gate[j], moe_w_up[j], moe_w_down[j])
        mk_p, mv_p = memory_kv(mem_prompt, norm_memkv[l], w_mk[l], w_mv[l])
        hp, st = decoder_layer(hp, pos_p, mk_p, mv_p,
                               jnp.zeros((bp, POOL_STATE, D_POOL), dt),
                               jnp.zeros((bp, CONV_W - 1, D_RNN), dt),
                               jnp.zeros((bp, D_RNN), dt), None, None, lw, ffn)
        pp.append(st[0]); pcv.append(st[1]); ph.append(st[2]); pc.append(st[3]); pk.append(st[4])
        pmk.append(mk_p); pmv.append(mv_p)
        ckv_past = cache_ckv[l][page_table].reshape(bsm, -1, R_KV)
        kr_past = cache_krope[l][page_table].reshape(bsm, -1, D_ROPE)
        hs, st = decoder_layer(hs, pos_s, cache_mem_k[l], cache_mem_v[l], state_pool[l], state_conv[l],
                               state_h[l], ckv_past, kr_past, lw, ffn)
        sp.append(st[0]); scv.append(st[1]); sh.append(st[2]); sc.append(st[3]); sk.append(st[4])
    y_prompt = rmsnorm(hp, norm_final)
    y_sample = rmsnorm(hs, norm_final)
    p_ckv = jnp.stack(pc)
    p_krope = jnp.stack(pk)
    p_pool = jnp.stack(pp)
    p_conv = jnp.stack(pcv)
    p_h = jnp.stack(ph)
    p_mem_k = jnp.stack(pmk)
    p_mem_v = jnp.stack(pmv)
    s_ckv = jnp.stack(sc)
    s_krope = jnp.stack(sk)
    s_pool = jnp.stack(sp)
    s_conv = jnp.stack(scv)
    s_h = jnp.stack(sh)
    return (y_prompt, y_sample, p_ckv, p_krope, p_pool, p_conv, p_h, p_mem_k, p_mem_v,
            s_ckv, s_krope, s_pool, s_conv, s_h)
```

```python
import functools

import jax
import jax.numpy as jnp
from jax import lax
from jax.experimental import pallas as pl
from jax.experimental.pallas import tpu as pltpu

F32 = jnp.float32
BF16 = jnp.bfloat16

EPS = 1e-6
POOL_WINDOWS = (2, 4, 8, 16)
POOL_STATE = max(POOL_WINDOWS) - 1
CONV_W = 4
LRU_C = 8.0
ROPE_BASE = 10000.0
D_NOPE = 128
D_ROPE = 64
D_V = 128
LANES = 128
VMEM_LIMIT = 56 * 1024 * 1024
NEG = -1e30


def _cparams(*sem):
    return pltpu.CompilerParams(dimension_semantics=sem, vmem_limit_bytes=VMEM_LIMIT)


def _resident(shape):
    nd = len(shape)
    return pl.BlockSpec(shape, lambda *_: (0,) * nd, pipeline_mode=pl.Buffered(1))


def _rms(x, g):
    ms = jnp.mean(x * x, axis=-1, keepdims=True)
    return x * lax.rsqrt(ms + EPS) * g


def _dot(a, b):
    return jnp.dot(a, b, preferred_element_type=F32)


def _dot_nt(a, b):
    return lax.dot_general(a, b, (((1,), (1,)), ((), ())), preferred_element_type=F32)


def _rope64(x, cos2, sin2):
    half = x.shape[-1] // 2
    rot = jnp.concatenate([x[:, half:], x[:, :half]], axis=-1)
    return x * cos2 + rot * sin2


def _silu(x):
    return x * jax.nn.sigmoid(x)


def _gelu_tanh(x):
    return 0.5 * x * (1.0 + jnp.tanh(0.7978845608028654 * (x + 0.044715 * (x * x * x))))


def _softplus(x):
    return jnp.maximum(x, 0.0) + jnp.log1p(jnp.exp(-jnp.abs(x)))


def _in_proj_kernel(x_ref, g_ref, w_ref, qn_ref, wuq_ref, kvn_ref, cos_ref, sin_ref,
                    u_ref, xb_ref, gb_ref, q_ref, ckv_ref, kr_ref, *, offs):
    o_pool, o_rx, o_rg, o_cq, o_ckv, o_kr, o_end = offs
    xn = _rms(x_ref[...], g_ref[...]).astype(BF16)

    def mm(a, b):
        return _dot(xn, w_ref[:, a:b])

    u_ref[...] = mm(o_pool, o_rx)
    xb_ref[...] = mm(o_rx, o_rg)
    gb_ref[...] = mm(o_rg, o_cq)
    cq = mm(o_cq, o_ckv)
    q_ref[...] = _dot(_rms(cq, qn_ref[...]).astype(BF16), wuq_ref[...])
    kv = mm(o_ckv, o_end)
    r_kv = o_kr - o_ckv
    ckv_ref[...] = _rms(kv[:, :r_kv], kvn_ref[...])
    kr_ref[...] = _rope64(kv[:, r_kv:r_kv + D_ROPE], cos_ref[...], sin_ref[...])


def _in_proj(x, g, w_in_p, q_norm, w_uq_p, kv_norm, cos2, sin2, offs, tm):
    T, D = x.shape
    n_in = w_in_p.shape[1]
    d_pool = offs[1] - offs[0]
    d_rnn = offs[2] - offs[1]
    r_q = offs[4] - offs[3]
    r_kv = offs[5] - offs[4]
    nq = w_uq_p.shape[1]
    row = lambda w: pl.BlockSpec((tm, w), lambda i: (i, 0))
    return pl.pallas_call(
        functools.partial(_in_proj_kernel, offs=offs),
        grid=(T // tm,),
        in_specs=[row(D), _resident((1, D)), _resident((D, n_in)), _resident((1, r_q)),
                  _resident((r_q, nq)), _resident((1, r_kv)), row(D_ROPE), row(D_ROPE)],
        out_specs=[row(d_pool), row(d_rnn), row(d_rnn), row(nq), row(r_kv), row(D_ROPE)],
        out_shape=[jax.ShapeDtypeStruct((T, w), F32) for w in (d_pool, d_rnn, d_rnn, nq, r_kv, D_ROPE)],
        compiler_params=_cparams("parallel"),
        name="in_proj",
    )(x, g, w_in_p, q_norm, w_uq_p, kv_norm, cos2, sin2)


def _norm_mm_kernel(x_ref, g_ref, w_ref, *o_refs, splits):
    xn = _rms(x_ref[...], g_ref[...]).astype(BF16)
    for o_ref, (a, b) in zip(o_refs, splits):
        o_ref[...] = _dot(xn, w_ref[:, a:b])


def _norm_mm(x, g, w, splits, tm, name):
    T, K = x.shape
    return pl.pallas_call(
        functools.partial(_norm_mm_kernel, splits=splits),
        grid=(T // tm,),
        in_specs=[pl.BlockSpec((tm, K), lambda i: (i, 0)), _resident((1, K)), _resident(w.shape)],
        out_specs=[pl.BlockSpec((tm, b - a), lambda i: (i, 0)) for a, b in splits],
        out_shape=[jax.ShapeDtypeStruct((T, b - a), F32) for a, b in splits],
        compiler_params=_cparams("parallel"),
        name=name,
    )(x, g, w)


def _mm_res_kernel(*refs, n, n_ptiles):
    ap_refs, as_refs, w_refs = refs[:n], refs[n:2 * n], refs[2 * n:3 * n]
    res_ref, o_ref = refs[3 * n], refs[3 * n + 1]
    i = pl.program_id(0)

    def body(a_refs):
        acc = res_ref[...]
        for a_ref, w_ref in zip(a_refs, w_refs):
            acc = acc + _dot(a_ref[...].astype(BF16), w_ref[...])
        o_ref[...] = acc

    @pl.when(i < n_ptiles)
    def _():
        body(ap_refs)

    @pl.when(i >= n_ptiles)
    def _():
        body(as_refs)


def _mm_res(a_pairs, w, res, tm, name):
    T, N = res.shape
    n = len(a_pairs)
    n_ptiles = a_pairs[0][0].shape[0] // tm
    n_stiles = a_pairs[0][1].shape[0] // tm
    assert n_ptiles + n_stiles == T // tm
    p_specs, s_specs, w_specs, r0 = [], [], [], 0
    for a_p, a_s in a_pairs:
        k = a_p.shape[1]
        assert r0 % k == 0 and a_p.shape[0] == n_ptiles * tm and a_s.shape == (n_stiles * tm, k)
        p_specs.append(pl.BlockSpec((tm, k), lambda i: (jnp.minimum(i, n_ptiles - 1), 0)))
        s_specs.append(pl.BlockSpec((tm, k), lambda i: (jnp.maximum(i - n_ptiles, 0), 0)))
        w_specs.append(pl.BlockSpec((k, N), functools.partial(lambda blk, i: (blk, 0), r0 // k),
                                    pipeline_mode=pl.Buffered(1)))
        r0 += k
    return pl.pallas_call(
        functools.partial(_mm_res_kernel, n=n, n_ptiles=n_ptiles),
        grid=(T // tm,),
        in_specs=p_specs + s_specs + w_specs + [pl.BlockSpec((tm, N), lambda i: (i, 0))],
        out_specs=pl.BlockSpec((tm, N), lambda i: (i, 0)),
        out_shape=jax.ShapeDtypeStruct((T, N), F32),
        compiler_params=_cparams("parallel"),
        name=name,
    )(*[a for a, _ in a_pairs], *[a for _, a in a_pairs], *([w] * n), res)


def _lru_coeffs(xc, wa_ref, ba_ref, wx_ref, bx_ref, sp):
    xcb = xc.astype(BF16)
    r = jax.nn.sigmoid(_dot(xcb, wa_ref[...]) + ba_ref[...])
    i = jax.nn.sigmoid(_dot(xcb, wx_ref[...]) + bx_ref[...])
    log_a = -LRU_C * r * sp
    a = jnp.exp(log_a)
    b = jnp.sqrt(jnp.tanh(-log_a) * (1.0 + a * a)) * (i * xc)
    return a, b


def _seqmix_prompt_kernel(u_ref, xb_ref, gb_ref, wpool_ref, pscale_ref, convw_ref, convb_ref,
                          wa_ref, ba_ref, wx_ref, bx_ref, lam_ref,
                          ypool_ref, yrnn_ref, pool_new_ref, conv_new_ref, h_new_ref,
                          ext_ref, extc_ref, a_ref, b_ref, hc_ref, *, TL, CH):
    l = pl.program_id(1)
    PH, CHist = 16, 8

    @pl.when(l == 0)
    def _():
        ext_ref[0:PH, :] = jnp.zeros((PH, ext_ref.shape[1]), F32)
        extc_ref[0:CHist, :] = jnp.zeros((CHist, extc_ref.shape[1]), F32)
        hc_ref[...] = jnp.zeros(hc_ref.shape, F32)

    @pl.when(l > 0)
    def _():
        ext_ref[0:PH, :] = ext_ref[TL:TL + PH, :]
        extc_ref[0:CHist, :] = extc_ref[TL:TL + CHist, :]

    ext_ref[PH:PH + TL, :] = u_ref[...]
    extc_ref[CHist:CHist + TL, :] = xb_ref[...]

    sp = _softplus(-lam_ref[...])
    gw = LANES
    for c in range(TL // CH):
        r0 = c * CH
        t = l * TL + r0 + lax.broadcasted_iota(jnp.int32, (CH, 1), 0)
        for g, w in enumerate(POOL_WINDOWS):
            cs = slice(g * gw, (g + 1) * gw)
            acc = ext_ref[PH + r0:PH + r0 + CH, cs]
            for k in range(1, w):
                acc = acc + ext_ref[PH + r0 - k:PH + r0 - k + CH, cs]
            cnt = jnp.minimum(t + 1, w).astype(F32)
            p = acc / cnt - u_ref[r0:r0 + CH, cs]
            ypool_ref[r0:r0 + CH, cs] = _dot(p.astype(BF16), wpool_ref[g]) * pscale_ref[:, cs]
        xc = convb_ref[...] + convw_ref[0:1, :] * extc_ref[CHist + r0 - (CONV_W - 1):CHist + r0 - (CONV_W - 1) + CH, :]
        for k in range(1, CONV_W):
            o = CHist + r0 - (CONV_W - 1) + k
            xc = xc + convw_ref[k:k + 1, :] * extc_ref[o:o + CH, :]
        a, b = _lru_coeffs(xc, wa_ref, ba_ref, wx_ref, bx_ref, sp)
        a_ref[r0:r0 + CH, :] = a
        b_ref[r0:r0 + CH, :] = b

    def step(i, h):
        o = pl.multiple_of(i * 8, 8)
        a8 = a_ref[pl.ds(o, 8), :]
        b8 = b_ref[pl.ds(o, 8), :]
        rows = []
        for r in range(8):
            h = a8[r:r + 1, :] * h + b8[r:r + 1, :]
            rows.append(h)
        b_ref[pl.ds(o, 8), :] = jnp.concatenate(rows, axis=0)
        return h

    h_last = lax.fori_loop(0, TL // 8, step, hc_ref[0:1, :])
    hc_ref[0:1, :] = h_last
    yrnn_ref[...] = _gelu_tanh(gb_ref[...]) * b_ref[...]
    pool_new_ref[...] = ext_ref[PH + TL - POOL_STATE:PH + TL, :]
    conv_new_ref[...] = extc_ref[CHist + TL - (CONV_W - 1):CHist + TL, :]
    h_new_ref[...] = h_last


def _seqmix_prompt(u, xb, gb, lw, B, L, TL):
    T, C = u.shape
    nl = L // TL
    CH = min(128, TL)
    seq = lambda w: pl.BlockSpec((TL, w), lambda b, l: (b * nl + l, 0))
    state = lambda r: pl.BlockSpec((None, r, C), lambda b, l: (b, 0, 0))
    return pl.pallas_call(
        functools.partial(_seqmix_prompt_kernel, TL=TL, CH=CH),
        grid=(B, nl),
        in_specs=[seq(C), seq(C), seq(C), _resident(lw['w_pool'].shape), _resident((1, C)),
                  _resident((CONV_W, C)), _resident((1, C)), _resident((C, C)), _resident((1, C)),
                  _resident((C, C)), _resident((1, C)), _resident((1, C))],
        out_specs=[seq(C), seq(C), state(POOL_STATE), state(CONV_W - 1), state(1)],
        out_shape=[jax.ShapeDtypeStruct((B * L, C), F32), jax.ShapeDtypeStruct((B * L, C), F32),
                   jax.ShapeDtypeStruct((B, POOL_STATE, C), F32),
                   jax.ShapeDtypeStruct((B, CONV_W - 1, C), F32),
                   jax.ShapeDtypeStruct((B, 1, C), F32)],
        scratch_shapes=[pltpu.VMEM((16 + TL, C), F32), pltpu.VMEM((8 + TL, C), F32),
                        pltpu.VMEM((TL, C), F32), pltpu.VMEM((TL, C), F32), pltpu.VMEM((8, C), F32)],
        compiler_params=_cparams("parallel", "arbitrary"),
        name="seqmix_prompt",
    )(u, xb, gb, lw['w_pool'], lw['pool_scale'], lw['conv_w'], lw['conv_b'],
      lw['w_rg_a_bd'], lw['b_rg_a'], lw['w_rg_x_bd'], lw['b_rg_x'], lw['lru_lambda'])


def _seqmix_sample_kernel(u_ref, xb_ref, gb_ref, sp_ref, sc_ref, h0_ref, wpool_ref, pscale_ref,
                          convw_ref, convb_ref, wa_ref, ba_ref, wx_ref, bx_ref, lam_ref,
                          ypool_ref, yrnn_ref, h_new_ref, *, LQ, pos0):
    gw = LANES
    sp = _softplus(-lam_ref[...])

    def ext(j, cs):
        return sp_ref[j, :, cs] if j < POOL_STATE else u_ref[j - POOL_STATE, :, cs]

    def extc(j):
        return sc_ref[j] if j < CONV_W - 1 else xb_ref[j - (CONV_W - 1)]

    h = h0_ref[...]
    for t in range(LQ):
        for g, w in enumerate(POOL_WINDOWS):
            cs = slice(g * gw, (g + 1) * gw)
            e = POOL_STATE + t
            acc = ext(e, cs)
            for k in range(1, w):
                acc = acc + ext(e - k, cs)
            cnt = float(min(pos0 + t + 1, w))
            p = acc / cnt - u_ref[t, :, cs]
            ypool_ref[t, :, cs] = _dot(p.astype(BF16), wpool_ref[g]) * pscale_ref[:, cs]
        xc = convb_ref[...] + convw_ref[0:1, :] * extc(t)
        for k in range(1, CONV_W):
            xc = xc + convw_ref[k:k + 1, :] * extc(t + k)
        a, b = _lru_coeffs(xc, wa_ref, ba_ref, wx_ref, bx_ref, sp)
        h = a * h + b
        yrnn_ref[t] = _gelu_tanh(gb_ref[t]) * h
    h_new_ref[...] = h


def _seqmix_sample(u, xb, gb, st_pool, st_conv, h0, lw, pos0):
    LQ, B, C = u.shape
    BT = B // 2 if B % 16 == 0 else B
    tb = lambda r: pl.BlockSpec((r, BT, C), lambda i: (0, i, 0))
    return pl.pallas_call(
        functools.partial(_seqmix_sample_kernel, LQ=LQ, pos0=pos0),
        grid=(B // BT,),
        in_specs=[tb(LQ), tb(LQ), tb(LQ), tb(POOL_STATE), tb(CONV_W - 1),
                  pl.BlockSpec((BT, C), lambda i: (i, 0)),
                  _resident(lw['w_pool'].shape), _resident((1, C)), _resident((CONV_W, C)),
                  _resident((1, C)), _resident((C, C)), _resident((1, C)), _resident((C, C)),
                  _resident((1, C)), _resident((1, C))],
        out_specs=[tb(LQ), tb(LQ), pl.BlockSpec((BT, C), lambda i: (i, 0))],
        out_shape=[jax.ShapeDtypeStruct((LQ, B, C), F32), jax.ShapeDtypeStruct((LQ, B, C), F32),
                   jax.ShapeDtypeStruct((B, C), F32)],
        compiler_params=_cparams("parallel"),
        name="seqmix_sample",
    )(u, xb, gb, st_pool, st_conv, h0, lw['w_pool'], lw['pool_scale'], lw['conv_w'], lw['conv_b'],
      lw['w_rg_a_bd'], lw['b_rg_a'], lw['w_rg_x_bd'], lw['b_rg_x'], lw['lru_lambda'])


def _absorb_queries(q_ref, cos, sin, wuk_ref, qa_s, qr_s, H, rows, scale):
    n_nope = H * D_NOPE
    for h in range(H):
        qn = q_ref[:, h * D_NOPE:(h + 1) * D_NOPE].astype(BF16)
        qa = _dot(qn, wuk_ref[h]) * scale
        qr = _rope64(q_ref[:, n_nope + h * D_ROPE:n_nope + (h + 1) * D_ROPE], cos, sin) * scale
        if qa_s.ndim == 3:
            qa_s[h] = qa.astype(BF16)
            qr_s[h] = qr.astype(BF16)
        else:
            qa_s[h * rows:(h + 1) * rows, :] = qa.astype(BF16)
            qr_s[h * rows:(h + 1) * rows, :] = qr.astype(BF16)


def _attn_prompt_kernel(q_ref, cos_ref, sin_ref, ckv_ref, kr_ref, wuk_ref, wuv_ref, y_ref,
                        qa_s, qr_s, m_s, l_s, acc_s, *, H, scale):
    qi, ki = pl.program_id(1), pl.program_id(2)
    tq, tk = q_ref.shape[0], ckv_ref.shape[0]

    @pl.when(ki == 0)
    def _():
        _absorb_queries(q_ref, cos_ref[...], sin_ref[...], wuk_ref, qa_s, qr_s, H, tq, scale)
        m_s[...] = jnp.full(m_s.shape, NEG, F32)
        l_s[...] = jnp.zeros(l_s.shape, F32)
        acc_s[...] = jnp.zeros(acc_s.shape, F32)

    def update(masked):
        k = ckv_ref[...].astype(BF16)
        r = kr_ref[...].astype(BF16)
        if masked:
            keep = (lax.broadcasted_iota(jnp.int32, (tq, tk), 1)
                    <= lax.broadcasted_iota(jnp.int32, (tq, tk), 0))

        def head(h, carry):
            s = _dot_nt(qa_s[h], k) + _dot_nt(qr_s[h], r)
            if masked:
                s = jnp.where(keep, s, NEG)
            m_prev = m_s[h]
            m_new = jnp.maximum(m_prev, jnp.max(s, axis=-1, keepdims=True))
            alpha = jnp.exp(m_prev - m_new)
            p = jnp.exp(s - m_new)
            l_s[h] = alpha * l_s[h] + jnp.sum(p, axis=-1, keepdims=True)
            acc_s[h] = alpha * acc_s[h] + _dot(p.astype(BF16), k)
            m_s[h] = m_new
            return carry

        lax.fori_loop(0, H, head, 0)

    @pl.when(ki < qi)
    def _():
        update(False)

    @pl.when(ki == qi)
    def _():
        update(True)
        for h in range(H):
            o = acc_s[h] / l_s[h]
            y_ref[:, h * D_V:(h + 1) * D_V] = _dot(o.astype(BF16), wuv_ref[h])


def _attn_prompt(q, cos2, sin2, ckv, kr, wuk_t, wuv_h, B, L, tq):
    T = q.shape[0]
    H, _, r_kv = wuk_t.shape
    nq = L // tq
    scale = float((D_NOPE + D_ROPE) ** -0.5)
    qrow = lambda w: pl.BlockSpec((tq, w), lambda b, i, j: (b * nq + i, 0))
    krow = lambda w: pl.BlockSpec((tq, w), lambda b, i, j: (b * nq + jnp.minimum(i, j), 0))
    return pl.pallas_call(
        functools.partial(_attn_prompt_kernel, H=H, scale=scale),
        grid=(B, nq, nq),
        in_specs=[qrow(q.shape[1]), qrow(D_ROPE), qrow(D_ROPE), krow(r_kv), krow(D_ROPE),
                  _resident(wuk_t.shape), _resident(wuv_h.shape)],
        out_specs=qrow(H * D_V),
        out_shape=jax.ShapeDtypeStruct((B * L, H * D_V), F32),
        scratch_shapes=[pltpu.VMEM((H, tq, r_kv), BF16), pltpu.VMEM((H, tq, D_ROPE), BF16),
                        pltpu.VMEM((H, tq, 1), F32), pltpu.VMEM((H, tq, 1), F32),
                        pltpu.VMEM((H, tq, r_kv), F32)],
        compiler_params=_cparams("parallel", "parallel", "arbitrary"),
        name="attn_prompt",
    )(q, cos2, sin2, ckv, kr, wuk_t, wuv_h)


def _attn_sample_kernel(pt_ref, q_ref, cos_ref, sin_ref, cnew_ref, rnew_ref, wuk_ref, wuv_ref,
                        ckv_hbm, kr_hbm, y_ref,
                        kbuf, rbuf, ksem, rsem, qa_s, qr_s, m_s, l_s, acc_s,
                        *, layer, G, H, LQ, PS, scale):
    b, j = pl.program_id(0), pl.program_id(1)
    nb, nch = pl.num_programs(0), pl.num_programs(1)
    c = b * nch + j
    slot = c % 2

    def page_copies(bb, jj, sl):
        copies = []
        for g in range(G):
            pg = pt_ref[bb, jj * G + g]
            copies.append(pltpu.make_async_copy(ckv_hbm.at[layer, pg], kbuf.at[sl, pl.ds(g * PS, PS)],
                                                ksem.at[sl]))
            copies.append(pltpu.make_async_copy(kr_hbm.at[layer, pg], rbuf.at[sl, pl.ds(g * PS, PS)],
                                                rsem.at[sl]))
        return copies

    @pl.when(c == 0)
    def _():
        for cp in page_copies(b, j, slot):
            cp.start()

    @pl.when(c + 1 < nb * nch)
    def _():
        last = j == nch - 1
        for cp in page_copies(jnp.where(last, b + 1, b), jnp.where(last, 0, j + 1), 1 - slot):
            cp.start()

    @pl.when(j == 0)
    def _():
        _absorb_queries(q_ref, cos_ref[...], sin_ref[...], wuk_ref, qa_s, qr_s, H, LQ, scale)
        m_s[...] = jnp.full(m_s.shape, NEG, F32)
        l_s[...] = jnp.zeros(l_s.shape, F32)
        acc_s[...] = jnp.zeros(acc_s.shape, F32)

    for cp in page_copies(b, j, slot):
        cp.wait()

    k = kbuf[slot].astype(BF16)
    s = _dot_nt(qa_s[...], k) + _dot_nt(qr_s[...], rbuf[slot].astype(BF16))
    m_prev = m_s[...]
    m_new = jnp.maximum(m_prev, jnp.max(s, axis=-1, keepdims=True))
    alpha = jnp.exp(m_prev - m_new)
    p = jnp.exp(s - m_new)
    l_s[...] = alpha * l_s[...] + jnp.sum(p, axis=-1, keepdims=True)
    acc_s[...] = alpha * acc_s[...] + _dot(p.astype(BF16), k)
    m_s[...] = m_new

    @pl.when(j == nch - 1)
    def _():
        rows = H * LQ
        qpos = lax.broadcasted_iota(jnp.int32, (rows, 1), 0) % LQ
        qa = qa_s[...].astype(F32)
        qr = qr_s[...].astype(F32)
        cols = []
        for t in range(LQ):
            st = (jnp.sum(qa * cnew_ref[t:t + 1, :], axis=-1, keepdims=True)
                  + jnp.sum(qr * rnew_ref[t:t + 1, :], axis=-1, keepdims=True))
            cols.append(jnp.where(qpos >= t, st, NEG))
        m_prev = m_s[...]
        m_new = m_prev
        for st in cols:
            m_new = jnp.maximum(m_new, st)
        alpha = jnp.exp(m_prev - m_new)
        lsum = alpha * l_s[...]
        acc = alpha * acc_s[...]
        for t, st in enumerate(cols):
            pt = jnp.exp(st - m_new)
            lsum = lsum + pt
            acc = acc + pt * cnew_ref[t:t + 1, :]
        o = (acc / lsum).astype(BF16)
        for h in range(H):
            y_ref[:, h * D_V:(h + 1) * D_V] = _dot(o[h * LQ:(h + 1) * LQ, :], wuv_ref[h])


def _attn_sample(page_table, q, cos2, sin2, cnew, rnew, wuk_t, wuv_h, cache_ckv, cache_krope, layer, G):
    B, LQ, nq = q.shape
    H, _, r_kv = wuk_t.shape
    PS = cache_ckv.shape[2]
    n_pages = page_table.shape[1]
    nch = n_pages // G
    scale = float((D_NOPE + D_ROPE) ** -0.5)
    seq = lambda w: pl.BlockSpec((None, LQ, w), lambda b, j, pt: (b, 0, 0))
    const = lambda shape: pl.BlockSpec(shape, lambda b, j, pt: (0,) * len(shape), pipeline_mode=pl.Buffered(1))
    grid_spec = pltpu.PrefetchScalarGridSpec(
        num_scalar_prefetch=1,
        grid=(B, nch),
        in_specs=[seq(nq), const((LQ, D_ROPE)), const((LQ, D_ROPE)), seq(r_kv), seq(D_ROPE),
                  const(wuk_t.shape), const(wuv_h.shape),
                  pl.BlockSpec(memory_space=pl.ANY), pl.BlockSpec(memory_space=pl.ANY)],
        out_specs=seq(H * D_V),
        scratch_shapes=[pltpu.VMEM((2, G * PS, r_kv), F32), pltpu.VMEM((2, G * PS, D_ROPE), F32),
                        pltpu.SemaphoreType.DMA((2,)), pltpu.SemaphoreType.DMA((2,)),
                        pltpu.VMEM((H * LQ, r_kv), BF16), pltpu.VMEM((H * LQ, D_ROPE), BF16),
                        pltpu.VMEM((H * LQ, 1), F32), pltpu.VMEM((H * LQ, 1), F32),
                        pltpu.VMEM((H * LQ, r_kv), F32)],
    )
    return pl.pallas_call(
        functools.partial(_attn_sample_kernel, layer=layer, G=G, H=H, LQ=LQ, PS=PS, scale=scale),
        grid_spec=grid_spec,
        out_shape=jax.ShapeDtypeStruct((B, LQ, H * D_V), F32),
        compiler_params=_cparams("arbitrary", "arbitrary"),
        name="attn_sample",
    )(page_table, q, cos2, sin2, cnew, rnew, wuk_t, wuv_h, cache_ckv, cache_krope)


def _mem_attn_kernel(q_ref, mk_ref, mv_ref, o_ref, *, NB, LQ, MH, HD, scale):
    for n in range(NB):
        rows = slice(n * LQ, (n + 1) * LQ)
        for h in range(MH):
            cs = slice(h * HD, (h + 1) * HD)
            q = (q_ref[rows, cs] * scale).astype(BF16)
            s = _dot_nt(q, mk_ref[n, :, cs].astype(BF16))
            m = jnp.max(s, axis=-1, keepdims=True)
            p = jnp.exp(s - m)
            denom = jnp.sum(p, axis=-1, keepdims=True)
            o_ref[rows, cs] = _dot(p.astype(BF16), mv_ref[n, :, cs].astype(BF16)) / denom


def _mem_attn(q, mk, mv, row0, n_seq, LQ, NB, mem_index, T_out, name):
    C = q.shape[1]
    M = mk.shape[-2]
    HD = LANES
    MH = C // HD
    rb = NB * LQ
    assert row0 % rb == 0 and n_seq % NB == 0
    qspec = pl.BlockSpec((rb, C), lambda i: (row0 // rb + i, 0))
    lead = mk.ndim - 3
    mspec = pl.BlockSpec((None,) * lead + (NB, M, C), mem_index)
    return pl.pallas_call(
        functools.partial(_mem_attn_kernel, NB=NB, LQ=LQ, MH=MH, HD=HD, scale=float(HD ** -0.5)),
        grid=(n_seq // NB,),
        in_specs=[qspec, mspec, mspec],
        out_specs=pl.BlockSpec((rb, C), lambda i: (i, 0)),
        out_shape=jax.ShapeDtypeStruct((T_out, C), F32),
        compiler_params=_cparams("parallel"),
        name=name,
    )(q, mk, mv)


def _ffn_kernel(x_ref, g_ref, wg_ref, wu_ref, wd_ref, o_ref, xn_s):
    f = pl.program_id(1)

    @pl.when(f == 0)
    def _():
        x = x_ref[...]
        xn_s[...] = _rms(x, g_ref[...]).astype(BF16)
        o_ref[...] = x

    xn = xn_s[...]
    h = _silu(_dot(xn, wg_ref[...])) * _dot(xn, wu_ref[...])
    o_ref[...] += _dot(h.astype(BF16), wd_ref[...])


def _ffn(x, g, wg, wu, wd, tm, tf):
    T, D = x.shape
    F = wg.shape[1]
    return pl.pallas_call(
        _ffn_kernel,
        grid=(T // tm, F // tf),
        in_specs=[pl.BlockSpec((tm, D), lambda i, f: (i, 0)), _resident((1, D)),
                  pl.BlockSpec((D, tf), lambda i, f: (0, f)), pl.BlockSpec((D, tf), lambda i, f: (0, f)),
                  pl.BlockSpec((tf, D), lambda i, f: (f, 0))],
        out_specs=pl.BlockSpec((tm, D), lambda i, f: (i, 0)),
        out_shape=jax.ShapeDtypeStruct((T, D), F32),
        scratch_shapes=[pltpu.VMEM((tm, D), BF16)],
        compiler_params=_cparams("parallel", "arbitrary"),
        name="ffn_dense",
    )(x, g, wg, wu, wd)


def _router_kernel(x_ref, g_ref, wr_ref, xn_ref, gates_ref, *, E):
    xn = _rms(x_ref[...], g_ref[...])
    xn_ref[...] = xn.astype(BF16)
    logits = jnp.dot(xn, wr_ref[...], preferred_element_type=F32, precision=lax.Precision.HIGHEST)
    lane = lax.broadcasted_iota(jnp.int32, logits.shape, 1).astype(F32)
    valid = lane < E
    logits = jnp.where(valid, logits, NEG)
    ex = jnp.exp(logits - jnp.max(logits, axis=-1, keepdims=True))
    probs = jnp.where(valid, ex / jnp.sum(ex, axis=-1, keepdims=True), -1.0)
    big = 1e9
    p1 = jnp.max(probs, axis=-1, keepdims=True)
    i1 = jnp.min(jnp.where(probs == p1, lane, big), axis=-1, keepdims=True)
    rest = jnp.where(lane == i1, -1.0, probs)
    p2 = jnp.max(rest, axis=-1, keepdims=True)
    i2 = jnp.min(jnp.where(rest == p2, lane, big), axis=-1, keepdims=True)
    tot = p1 + p2
    gates = jnp.where(lane == i1, p1 / tot, 0.0) + jnp.where(lane == i2, p2 / tot, 0.0)
    gates_ref[...] = gates[:, :gates_ref.shape[1]]


def _router(x, g, wr_p, E, tm):
    T, D = x.shape
    return pl.pallas_call(
        functools.partial(_router_kernel, E=E),
        grid=(T // tm,),
        in_specs=[pl.BlockSpec((tm, D), lambda i: (i, 0)), _resident((1, D)), _resident(wr_p.shape)],
        out_specs=[pl.BlockSpec((tm, D), lambda i: (i, 0)), pl.BlockSpec((tm, LANES), lambda i: (i, 0))],
        out_shape=[jax.ShapeDtypeStruct((T, D), BF16), jax.ShapeDtypeStruct((T, LANES), F32)],
        compiler_params=_cparams("parallel"),
        name="moe_router",
    )(x, g, wr_p)


def _moe_dense_kernel(xn_ref, x_ref, gate_ref, wg_ref, wu_ref, wd_ref, o_ref):
    e, f = pl.program_id(1), pl.program_id(2)

    @pl.when((e == 0) & (f == 0))
    def _():
        o_ref[...] = x_ref[...]

    xn = xn_ref[...]
    h = _silu(_dot(xn, wg_ref[...])) * _dot(xn, wu_ref[...])
    o_ref[...] += gate_ref[...] * _dot(h.astype(BF16), wd_ref[...])


def _moe_dense(xn, x, gates_t, wg, wu, wd, tm, tf):
    T, D = x.shape
    E, _, F = wg.shape
    return pl.pallas_call(
        _moe_dense_kernel,
        grid=(T // tm, E, F // tf),
        in_specs=[pl.BlockSpec((tm, D), lambda i, e, f: (i, 0)),
                  pl.BlockSpec((tm, D), lambda i, e, f: (i, 0)),
                  pl.BlockSpec((None, tm, 1), lambda i, e, f: (e, i, 0)),
                  pl.BlockSpec((None, D, tf), lambda i, e, f: (e, 0, f)),
                  pl.BlockSpec((None, D, tf), lambda i, e, f: (e, 0, f)),
                  pl.BlockSpec((None, tf, D), lambda i, e, f: (e, f, 0))],
        out_specs=pl.BlockSpec((tm, D), lambda i, e, f: (i, 0)),
        out_shape=jax.ShapeDtypeStruct((T, D), F32),
        compiler_params=_cparams("parallel", "arbitrary", "arbitrary"),
        name="moe_dense",
    )(xn, x, gates_t, wg, wu, wd)


def _norm_kernel(x_ref, g_ref, o_ref):
    o_ref[...] = _rms(x_ref[...], g_ref[...])


def _final_norm(x, g, row0, n_rows, tm):
    D = x.shape[1]
    return pl.pallas_call(
        _norm_kernel,
        grid=(n_rows // tm,),
        in_specs=[pl.BlockSpec((tm, D), lambda i: (row0 // tm + i, 0)), _resident((1, D))],
        out_specs=pl.BlockSpec((tm, D), lambda i: (i, 0)),
        out_shape=jax.ShapeDtypeStruct((n_rows, D), F32),
        compiler_params=_cparams("parallel"),
        name="final_norm",
    )(x, g)


def _block_diag(w):
    n, c, d = w.shape
    eye = jnp.eye(n, dtype=w.dtype)
    return (w[:, :, None, :] * eye[:, None, :, None]).reshape(n * c, n * d)


def _pad_to(x, axis, mult):
    pad = (-x.shape[axis]) % mult
    if pad == 0:
        return x
    widths = [(0, 0)] * x.ndim
    widths[axis] = (0, pad)
    return jnp.pad(x, widths)


def kernel(x_prompt, x_sample, mem_prompt, cache_ckv, cache_krope, page_table, state_pool, state_conv, state_h, cache_mem_k, cache_mem_v, norm_mix, norm_mem, norm_memkv, norm_ffn, norm_final, w_in, w_pool, pool_scale, conv_w, conv_b, w_rg_a, b_rg_a, w_rg_x, b_rg_x, lru_lambda, q_norm, w_uq, kv_norm, w_uk, w_uv, w_out, w_mq, w_mk, w_mv, w_mo, ffn_w_gate, ffn_w_up, ffn_w_down, moe_router, moe_w_gate, moe_w_up, moe_w_down):
    B, L, D = x_prompt.shape
    BS, LS, _ = x_sample.shape
    depth = w_in.shape[0]
    d_pool = state_pool.shape[-1]
    d_rnn = state_conv.shape[-1]
    r_q = q_norm.shape[-1]
    r_kv = kv_norm.shape[-1]
    H = w_uq.shape[2]
    M, MH, MHD = cache_mem_k.shape[2:]
    E = moe_router.shape[-1]
    PS = cache_ckv.shape[2]
    n_pages = page_table.shape[1]
    past_len = n_pages * PS
    NP, NS = B * L, BS * LS
    T = NP + NS
    assert d_pool == len(POOL_WINDOWS) * LANES and MHD == LANES and w_uq.shape[3] == D_NOPE + D_ROPE

    tm = min(512, NS)
    assert NP % tm == 0 and NS % tm == 0
    tq = min(512, L)
    TL = min(512, L)
    G = max(1, min(16, n_pages // 2))
    assert n_pages % G == 0
    offs = (0, d_pool, d_pool + d_rnn, d_pool + 2 * d_rnn, d_pool + 2 * d_rnn + r_q,
            d_pool + 2 * d_rnn + r_q + r_kv)
    n_in = offs[-1] + D_ROPE
    offs = offs + (offs[-1] + LANES,)

    half = D_ROPE // 2
    freqs = ROPE_BASE ** (-jnp.arange(half, dtype=F32) / half)
    pos_p = jnp.arange(L, dtype=jnp.int32)
    pos_s = past_len + jnp.arange(LS, dtype=jnp.int32)

    def tables(pos):
        ang = pos.astype(F32)[:, None] * freqs[None, :]
        c, s = jnp.cos(ang), jnp.sin(ang)
        return jnp.concatenate([c, c], -1), jnp.concatenate([-s, s], -1)

    cos_p, sin_p = tables(pos_p)
    cos_s, sin_s = tables(pos_s)
    cos_all = jnp.concatenate([jnp.tile(cos_p, (B, 1)), jnp.tile(cos_s, (BS, 1))], 0)
    sin_all = jnp.concatenate([jnp.tile(sin_p, (B, 1)), jnp.tile(sin_s, (BS, 1))], 0)

    x = jnp.concatenate([x_prompt.reshape(NP, D), x_sample.reshape(NS, D)], 0)
    mem2 = mem_prompt.reshape(B * M, D)
    cmk = cache_mem_k.reshape(depth, BS, M, MH * MHD)
    cmv = cache_mem_v.reshape(depth, BS, M, MH * MHD)
    row = lambda v: v.reshape(1, -1)

    outs = {k: [] for k in ('pc', 'pk', 'pp', 'pcv', 'ph', 'pmk', 'pmv', 'sc', 'sk', 'sp', 'scv', 'sh')}
    for l in range(depth):
        w_in_p = _pad_to(w_in[l], 1, LANES)
        w_in_p = jnp.pad(w_in_p, ((0, 0), (0, offs[-1] - w_in_p.shape[1]))).astype(BF16)
        wuq = w_uq[l]
        w_uq_p = jnp.concatenate([wuq[:, :, :D_NOPE].reshape(r_q, H * D_NOPE),
                                  wuq[:, :, D_NOPE:].reshape(r_q, H * D_ROPE)], 1).astype(BF16)
        wuk_t = jnp.transpose(w_uk[l], (1, 2, 0)).astype(BF16)
        wuv_h = jnp.transpose(w_uv[l], (1, 0, 2)).astype(BF16)
        lw = {
            'w_pool': w_pool[l].astype(BF16), 'pool_scale': row(pool_scale[l]),
            'conv_w': conv_w[l], 'conv_b': row(conv_b[l]),
            'w_rg_a_bd': _block_diag(w_rg_a[l]).astype(BF16), 'b_rg_a': row(b_rg_a[l]),
            'w_rg_x_bd': _block_diag(w_rg_x[l]).astype(BF16), 'b_rg_x': row(b_rg_x[l]),
            'lru_lambda': row(lru_lambda[l]),
        }

        u, xb, gb, q, ckv, kr = _in_proj(x, row(norm_mix[l]), w_in_p, row(q_norm[l]), w_uq_p,
                                         row(kv_norm[l]), cos_all, sin_all, offs, tm)

        y_pool, y_rnn, pool_new, conv_new, h_new = _seqmix_prompt(u, xb, gb, lw, B, L, TL)
        tmaj = lambda a: jnp.transpose(a[NP:].reshape(BS, LS, -1), (1, 0, 2))
        u_s, xb_s, gb_s = tmaj(u), tmaj(xb), tmaj(gb)
        yp_s, yr_s, h_s = _seqmix_sample(u_s, xb_s, gb_s, jnp.transpose(state_pool[l], (1, 0, 2)),
                                         jnp.transpose(state_conv[l], (1, 0, 2)), state_h[l], lw, past_len)
        bmaj = lambda a: jnp.transpose(a, (1, 0, 2)).reshape(NS, -1)

        y_mla = _attn_prompt(q, cos_all, sin_all, ckv, kr, wuk_t, wuv_h, B, L, tq)
        s_ckv = ckv[NP:].reshape(BS, LS, r_kv)
        s_kr = kr[NP:].reshape(BS, LS, D_ROPE)
        y_mla_s = _attn_sample(page_table, q[NP:].reshape(BS, LS, -1), cos_s, sin_s, s_ckv, s_kr,
                               wuk_t, wuv_h, cache_ckv, cache_krope, l, G)
        x = _mm_res([(y_pool, bmaj(yp_s)), (y_rnn, bmaj(yr_s)), (y_mla, y_mla_s.reshape(NS, -1))],
                    w_out[l].astype(BF16), x, tm, "out_proj")

        w_mkv = jnp.concatenate([w_mk[l].reshape(D, MH * MHD), w_mv[l].reshape(D, MH * MHD)], 1).astype(BF16)
        mk_p, mv_p = _norm_mm(mem2, row(norm_memkv[l]), w_mkv,
                              ((0, MH * MHD), (MH * MHD, 2 * MH * MHD)), min(512, B * M), "mem_kv")
        (qm,) = _norm_mm(x, row(norm_mem[l]), w_mq[l].reshape(D, MH * MHD).astype(BF16),
                         ((0, MH * MHD),), tm, "mem_q")
        lq_p = min(512, L)
        o_mem = _mem_attn(qm, mk_p.reshape(B, M, -1), mv_p.reshape(B, M, -1), 0, NP // lq_p, lq_p, 1,
                          lambda i: (i // (L // lq_p), 0, 0), NP, "mem_attn_prompt")
        nb_s = 8 if BS % 8 == 0 else 1
        o_mem_s = _mem_attn(qm, cmk, cmv, NP, BS, LS, nb_s, lambda i: (l, i, 0, 0), NS, "mem_attn_sample")
        x = _mm_res([(o_mem, o_mem_s)], w_mo[l].reshape(MH * MHD, D).astype(BF16), x, tm, "mem_out")

        j = l // 2
        if l % 2 == 0:
            x = _ffn(x, row(norm_ffn[l]), ffn_w_gate[j].astype(BF16), ffn_w_up[j].astype(BF16),
                     ffn_w_down[j].astype(BF16), tm, 512)
        else:
            wr_p = _pad_to(moe_router[j], 1, LANES)
            xn, gates = _router(x, row(norm_ffn[l]), wr_p, E, tm)
            gates_t = jnp.transpose(gates[:, :E])[:, :, None]
            tf = 512
            wg = _pad_to(moe_w_gate[j], 2, tf).astype(BF16)
            wu = _pad_to(moe_w_up[j], 2, tf).astype(BF16)
            wd = _pad_to(moe_w_down[j], 1, tf).astype(BF16)
            x = _moe_dense(xn, x, gates_t, wg, wu, wd, tm, tf)

        outs['pc'].append(ckv[:NP].reshape(B, L, r_kv))
        outs['pk'].append(kr[:NP].reshape(B, L, D_ROPE))
        outs['pp'].append(pool_new)
        outs['pcv'].append(conv_new)
        outs['ph'].append(h_new.reshape(B, d_rnn))
        outs['pmk'].append(mk_p.reshape(B, M, MH, MHD))
        outs['pmv'].append(mv_p.reshape(B, M, MH, MHD))
        outs['sc'].append(s_ckv)
        outs['sk'].append(s_kr)
        u_sb = u[NP:].reshape(BS, LS, d_pool)
        xb_sb = xb[NP:].reshape(BS, LS, d_rnn)
        outs['sp'].append(jnp.concatenate([state_pool[l], u_sb], 1)[:, -POOL_STATE:])
        outs['scv'].append(jnp.concatenate([state_conv[l], xb_sb], 1)[:, -(CONV_W - 1):])
        outs['sh'].append(h_s)

    y_prompt = _final_norm(x, row(norm_final), 0, NP, tm).reshape(B, L, D)
    y_sample = _final_norm(x, row(norm_final), NP, NS, tm).reshape(BS, LS, D)
    st = lambda k: jnp.stack(outs[k])
    return (y_prompt, y_sample, st('pc'), st('pk'), st('pp'), st('pcv'), st('ph'), st('pmk'), st('pmv'),
            st('sc'), st('sk'), st('sp'), st('scv'), st('sh'))
```

```python
import functools

import jax
import jax.numpy as jnp
from jax import lax
from jax.experimental import pallas as pl
from jax.experimental.pallas import tpu as pltpu

F32 = jnp.float32
BF16 = jnp.bfloat16

EPS = 1e-6
POOL_WINDOWS = (2, 4, 8, 16)
POOL_STATE = max(POOL_WINDOWS) - 1
CONV_W = 4
LRU_C = 8.0
ROPE_BASE = 10000.0
D_NOPE = 128
D_ROPE = 64
D_V = 128
LANES = 128
VMEM_LIMIT = 56 * 1024 * 1024
NEG = -1e30


def _cparams(*sem):
    return pltpu.CompilerParams(dimension_semantics=sem, vmem_limit_bytes=VMEM_LIMIT)


def _resident(shape):
    nd = len(shape)
    return pl.BlockSpec(shape, lambda *_: (0,) * nd, pipeline_mode=pl.Buffered(1))


def _rms(x, g):
    ms = jnp.mean(x * x, axis=-1, keepdims=True)
    return x * lax.rsqrt(ms + EPS) * g


def _dot(a, b):
    return jnp.dot(a, b, preferred_element_type=F32)


def _dot_nt(a, b):
    return lax.dot_general(a, b, (((1,), (1,)), ((), ())), preferred_element_type=F32)


def _rope64(x, cos2, sin2):
    half = x.shape[-1] // 2
    rot = jnp.concatenate([x[:, half:], x[:, :half]], axis=-1)
    return x * cos2 + rot * sin2


def _silu(x):
    return x * jax.nn.sigmoid(x)


def _gelu_tanh(x):
    return 0.5 * x * (1.0 + jnp.tanh(0.7978845608028654 * (x + 0.044715 * (x * x * x))))


def _softplus(x):
    return jnp.maximum(x, 0.0) + jnp.log1p(jnp.exp(-jnp.abs(x)))


def _in_proj_kernel(x_ref, g_ref, w_ref, qn_ref, wuq_ref, kvn_ref, cos_ref, sin_ref,
                    u_ref, xb_ref, gb_ref, q_ref, ckv_ref, kr_ref, *, offs):
    o_pool, o_rx, o_rg, o_cq, o_ckv, o_kr, o_end = offs
    xn = _rms(x_ref[...], g_ref[...]).astype(BF16)

    def mm(a, b):
        return _dot(xn, w_ref[:, a:b])

    u_ref[...] = mm(o_pool, o_rx)
    xb_ref[...] = mm(o_rx, o_rg)
    gb_ref[...] = mm(o_rg, o_cq)
    cq = mm(o_cq, o_ckv)
    q_ref[...] = _dot(_rms(cq, qn_ref[...]).astype(BF16), wuq_ref[...])
    kv = mm(o_ckv, o_end)
    r_kv = o_kr - o_ckv
    ckv_ref[...] = _rms(kv[:, :r_kv], kvn_ref[...])
    kr_ref[...] = _rope64(kv[:, r_kv:r_kv + D_ROPE], cos_ref[...], sin_ref[...])


def _in_proj(x, g, w_in_p, q_norm, w_uq_p, kv_norm, cos2, sin2, offs, tm):
    T, D = x.shape
    n_in = w_in_p.shape[1]
    d_pool = offs[1] - offs[0]
    d_rnn = offs[2] - offs[1]
    r_q = offs[4] - offs[3]
    r_kv = offs[5] - offs[4]
    nq = w_uq_p.shape[1]
    row = lambda w: pl.BlockSpec((tm, w), lambda i: (i, 0))
    return pl.pallas_call(
        functools.partial(_in_proj_kernel, offs=offs),
        grid=(T // tm,),
        in_specs=[row(D), _resident((1, D)), _resident((D, n_in)), _resident((1, r_q)),
                  _resident((r_q, nq)), _resident((1, r_kv)), row(D_ROPE), row(D_ROPE)],
        out_specs=[row(d_pool), row(d_rnn), row(d_rnn), row(nq), row(r_kv), row(D_ROPE)],
        out_shape=[jax.ShapeDtypeStruct((T, w), F32) for w in (d_pool, d_rnn, d_rnn, nq, r_kv, D_ROPE)],
        compiler_params=_cparams("parallel"),
        name="in_proj",
    )(x, g, w_in_p, q_norm, w_uq_p, kv_norm, cos2, sin2)


def _norm_mm_kernel(x_ref, g_ref, w_ref, *o_refs, splits):
    xn = _rms(x_ref[...], g_ref[...]).astype(BF16)
    for o_ref, (a, b) in zip(o_refs, splits):
        o_ref[...] = _dot(xn, w_ref[:, a:b])


def _norm_mm(x, g, w, splits, tm, name):
    T, K = x.shape
    return pl.pallas_call(
        functools.partial(_norm_mm_kernel, splits=splits),
        grid=(T // tm,),
        in_specs=[pl.BlockSpec((tm, K), lambda i: (i, 0)), _resident((1, K)), _resident(w.shape)],
        out_specs=[pl.BlockSpec((tm, b - a), lambda i: (i, 0)) for a, b in splits],
        out_shape=[jax.ShapeDtypeStruct((T, b - a), F32) for a, b in splits],
        compiler_params=_cparams("parallel"),
        name=name,
    )(x, g, w)


def _mm_res_kernel(*refs, n, n_ptiles):
    ap_refs, as_refs, w_refs = refs[:n], refs[n:2 * n], refs[2 * n:3 * n]
    res_ref, o_ref = refs[3 * n], refs[3 * n + 1]
    i = pl.program_id(0)

    def body(a_refs):
        acc = res_ref[...]
        for a_ref, w_ref in zip(a_refs, w_refs):
            acc = acc + _dot(a_ref[...].astype(BF16), w_ref[...])
        o_ref[...] = acc

    @pl.when(i < n_ptiles)
    def _():
        body(ap_refs)

    @pl.when(i >= n_ptiles)
    def _():
        body(as_refs)


def _mm_res(a_pairs, w, res, tm, name):
    T, N = res.shape
    n = len(a_pairs)
    n_ptiles = a_pairs[0][0].shape[0] // tm
    n_stiles = a_pairs[0][1].shape[0] // tm
    assert n_ptiles + n_stiles == T // tm
    p_specs, s_specs, w_specs, r0 = [], [], [], 0
    for a_p, a_s in a_pairs:
        k = a_p.shape[1]
        assert r0 % k == 0 and a_p.shape[0] == n_ptiles * tm and a_s.shape == (n_stiles * tm, k)
        p_specs.append(pl.BlockSpec((tm, k), lambda i: (jnp.minimum(i, n_ptiles - 1), 0)))
        s_specs.append(pl.BlockSpec((tm, k), lambda i: (jnp.maximum(i - n_ptiles, 0), 0)))
        w_specs.append(pl.BlockSpec((k, N), functools.partial(lambda blk, i: (blk, 0), r0 // k),
                                    pipeline_mode=pl.Buffered(1)))
        r0 += k
    return pl.pallas_call(
        functools.partial(_mm_res_kernel, n=n, n_ptiles=n_ptiles),
        grid=(T // tm,),
        in_specs=p_specs + s_specs + w_specs + [pl.BlockSpec((tm, N), lambda i: (i, 0))],
        out_specs=pl.BlockSpec((tm, N), lambda i: (i, 0)),
        out_shape=jax.ShapeDtypeStruct((T, N), F32),
        compiler_params=_cparams("parallel"),
        name=name,
    )(*[a for a, _ in a_pairs], *[a for _, a in a_pairs], *([w] * n), res)


def _lru_coeffs(xc, wa_ref, ba_ref, wx_ref, bx_ref, sp):
    xcb = xc.astype(BF16)
    r = jax.nn.sigmoid(_dot(xcb, wa_ref[...]) + ba_ref[...])
    i = jax.nn.sigmoid(_dot(xcb, wx_ref[...]) + bx_ref[...])
    log_a = -LRU_C * r * sp
    a = jnp.exp(log_a)
    b = jnp.sqrt(jnp.tanh(-log_a) * (1.0 + a * a)) * (i * xc)
    return a, b


def _seqmix_prompt_kernel(u_ref, xb_ref, gb_ref, wpool_ref, pscale_ref, convw_ref, convb_ref,
                          wa_ref, ba_ref, wx_ref, bx_ref, lam_ref,
                          ypool_ref, yrnn_ref, pool_new_ref, conv_new_ref, h_new_ref,
                          ext_ref, extc_ref, a_ref, b_ref, hc_ref, *, TL, CH):
    l = pl.program_id(1)
    PH, CHist = 16, 8

    @pl.when(l == 0)
    def _():
        ext_ref[0:PH, :] = jnp.zeros((PH, ext_ref.shape[1]), F32)
        extc_ref[0:CHist, :] = jnp.zeros((CHist, extc_ref.shape[1]), F32)
        hc_ref[...] = jnp.zeros(hc_ref.shape, F32)

    @pl.when(l > 0)
    def _():
        ext_ref[0:PH, :] = ext_ref[TL:TL + PH, :]
        extc_ref[0:CHist, :] = extc_ref[TL:TL + CHist, :]

    ext_ref[PH:PH + TL, :] = u_ref[...]
    extc_ref[CHist:CHist + TL, :] = xb_ref[...]

    sp = _softplus(-lam_ref[...])
    gw = LANES
    for c in range(TL // CH):
        r0 = c * CH
        t = l * TL + r0 + lax.broadcasted_iota(jnp.int32, (CH, 1), 0)
        for g, w in enumerate(POOL_WINDOWS):
            cs = slice(g * gw, (g + 1) * gw)
            acc = ext_ref[PH + r0:PH + r0 + CH, cs]
            for k in range(1, w):
                acc = acc + ext_ref[PH + r0 - k:PH + r0 - k + CH, cs]
            cnt = jnp.minimum(t + 1, w).astype(F32)
            p = acc / cnt - u_ref[r0:r0 + CH, cs]
            ypool_ref[r0:r0 + CH, cs] = _dot(p.astype(BF16), wpool_ref[g]) * pscale_ref[:, cs]
        xc = convb_ref[...] + convw_ref[0:1, :] * extc_ref[CHist + r0 - (CONV_W - 1):CHist + r0 - (CONV_W - 1) + CH, :]
        for k in range(1, CONV_W):
            o = CHist + r0 - (CONV_W - 1) + k
            xc = xc + convw_ref[k:k + 1, :] * extc_ref[o:o + CH, :]
        a, b = _lru_coeffs(xc, wa_ref, ba_ref, wx_ref, bx_ref, sp)
        a_ref[r0:r0 + CH, :] = a
        b_ref[r0:r0 + CH, :] = b

    def step(i, h):
        o = pl.multiple_of(i * 8, 8)
        a8 = a_ref[pl.ds(o, 8), :]
        b8 = b_ref[pl.ds(o, 8), :]
        rows = []
        for r in range(8):
            h = a8[r:r + 1, :] * h + b8[r:r + 1, :]
            rows.append(h)
        b_ref[pl.ds(o, 8), :] = jnp.concatenate(rows, axis=0)
        return h

    h_last = lax.fori_loop(0, TL // 8, step, hc_ref[0:1, :])
    hc_ref[0:1, :] = h_last
    yrnn_ref[...] = _gelu_tanh(gb_ref[...]) * b_ref[...]
    pool_new_ref[...] = ext_ref[PH + TL - POOL_STATE:PH + TL, :]
    conv_new_ref[...] = extc_ref[CHist + TL - (CONV_W - 1):CHist + TL, :]
    h_new_ref[...] = h_last


def _seqmix_prompt(u, xb, gb, lw, B, L, TL):
    T, C = u.shape
    nl = L // TL
    CH = min(128, TL)
    seq = lambda w: pl.BlockSpec((TL, w), lambda b, l: (b * nl + l, 0))
    state = lambda r: pl.BlockSpec((None, r, C), lambda b, l: (b, 0, 0))
    return pl.pallas_call(
        functools.partial(_seqmix_prompt_kernel, TL=TL, CH=CH),
        grid=(B, nl),
        in_specs=[seq(C), seq(C), seq(C), _resident(lw['w_pool'].shape), _resident((1, C)),
                  _resident((CONV_W, C)), _resident((1, C)), _resident((C, C)), _resident((1, C)),
                  _resident((C, C)), _resident((1, C)), _resident((1, C))],
        out_specs=[seq(C), seq(C), state(POOL_STATE), state(CONV_W - 1), state(1)],
        out_shape=[jax.ShapeDtypeStruct((B * L, C), F32), jax.ShapeDtypeStruct((B * L, C), F32),
                   jax.ShapeDtypeStruct((B, POOL_STATE, C), F32),
                   jax.ShapeDtypeStruct((B, CONV_W - 1, C), F32),
                   jax.ShapeDtypeStruct((B, 1, C), F32)],
        scratch_shapes=[pltpu.VMEM((16 + TL, C), F32), pltpu.VMEM((8 + TL, C), F32),
                        pltpu.VMEM((TL, C), F32), pltpu.VMEM((TL, C), F32), pltpu.VMEM((8, C), F32)],
        compiler_params=_cparams("parallel", "arbitrary"),
        name="seqmix_prompt",
    )(u, xb, gb, lw['w_pool'], lw['pool_scale'], lw['conv_w'], lw['conv_b'],
      lw['w_rg_a_bd'], lw['b_rg_a'], lw['w_rg_x_bd'], lw['b_rg_x'], lw['lru_lambda'])


def _seqmix_sample_kernel(u_ref, xb_ref, gb_ref, sp_ref, sc_ref, h0_ref, wpool_ref, pscale_ref,
                          convw_ref, convb_ref, wa_ref, ba_ref, wx_ref, bx_ref, lam_ref,
                          ypool_ref, yrnn_ref, h_new_ref, *, LQ, pos0):
    gw = LANES
    sp = _softplus(-lam_ref[...])

    def ext(j, cs):
        return sp_ref[j, :, cs] if j < POOL_STATE else u_ref[j - POOL_STATE, :, cs]

    def extc(j):
        return sc_ref[j] if j < CONV_W - 1 else xb_ref[j - (CONV_W - 1)]

    h = h0_ref[...]
    for t in range(LQ):
        for g, w in enumerate(POOL_WINDOWS):
            cs = slice(g * gw, (g + 1) * gw)
            e = POOL_STATE + t
            acc = ext(e, cs)
            for k in range(1, w):
                acc = acc + ext(e - k, cs)
            cnt = float(min(pos0 + t + 1, w))
            p = acc / cnt - u_ref[t, :, cs]
            ypool_ref[t, :, cs] = _dot(p.astype(BF16), wpool_ref[g]) * pscale_ref[:, cs]
        xc = convb_ref[...] + convw_ref[0:1, :] * extc(t)
        for k in range(1, CONV_W):
            xc = xc + convw_ref[k:k + 1, :] * extc(t + k)
        a, b = _lru_coeffs(xc, wa_ref, ba_ref, wx_ref, bx_ref, sp)
        h = a * h + b
        yrnn_ref[t] = _gelu_tanh(gb_ref[t]) * h
    h_new_ref[...] = h


def _seqmix_sample(u, xb, gb, st_pool, st_conv, h0, lw, pos0):
    LQ, B, C = u.shape
    BT = B // 2 if B % 16 == 0 else B
    tb = lambda r: pl.BlockSpec((r, BT, C), lambda i: (0, i, 0))
    return pl.pallas_call(
        functools.partial(_seqmix_sample_kernel, LQ=LQ, pos0=pos0),
        grid=(B // BT,),
        in_specs=[tb(LQ), tb(LQ), tb(LQ), tb(POOL_STATE), tb(CONV_W - 1),
                  pl.BlockSpec((BT, C), lambda i: (i, 0)),
                  _resident(lw['w_pool'].shape), _resident((1, C)), _resident((CONV_W, C)),
                  _resident((1, C)), _resident((C, C)), _resident((1, C)), _resident((C, C)),
                  _resident((1, C)), _resident((1, C))],
        out_specs=[tb(LQ), tb(LQ), pl.BlockSpec((BT, C), lambda i: (i, 0))],
        out_shape=[jax.ShapeDtypeStruct((LQ, B, C), F32), jax.ShapeDtypeStruct((LQ, B, C), F32),
                   jax.ShapeDtypeStruct((B, C), F32)],
        compiler_params=_cparams("parallel"),
        name="seqmix_sample",
    )(u, xb, gb, st_pool, st_conv, h0, lw['w_pool'], lw['pool_scale'], lw['conv_w'], lw['conv_b'],
      lw['w_rg_a_bd'], lw['b_rg_a'], lw['w_rg_x_bd'], lw['b_rg_x'], lw['lru_lambda'])


def _lane_fold(x, op):
    out = x[:, :LANES]
    for c in range(1, x.shape[1] // LANES):
        out = op(out, x[:, c * LANES:(c + 1) * LANES])
    return out


def _row_max(x):
    return jnp.max(_lane_fold(x, jnp.maximum), axis=-1, keepdims=True)


def _row_sum(x):
    return jnp.sum(_lane_fold(x, jnp.add), axis=-1, keepdims=True)


def _absorb_queries(q_ref, cos, sin, wuk_ref, qa_s, qr_s, H, rows, scale):
    n_nope = H * D_NOPE
    for h in range(H):
        qn = q_ref[:, h * D_NOPE:(h + 1) * D_NOPE].astype(BF16)
        qa = _dot(qn, wuk_ref[h]) * scale
        qr = _rope64(q_ref[:, n_nope + h * D_ROPE:n_nope + (h + 1) * D_ROPE], cos, sin) * scale
        if qa_s.ndim == 3:
            qa_s[h] = qa.astype(BF16)
            qr_s[h] = qr.astype(BF16)
        else:
            qa_s[h * rows:(h + 1) * rows, :] = qa.astype(BF16)
            qr_s[h * rows:(h + 1) * rows, :] = qr.astype(BF16)


def _attn_prompt_kernel(q_ref, cos_ref, sin_ref, ckv_ref, kr_ref, wuk_ref, wuv_ref, y_ref,
                        qa_s, qr_s, m_s, l_s, acc_s, *, H, scale):
    qi, ki = pl.program_id(1), pl.program_id(2)
    tq, tk = q_ref.shape[0], ckv_ref.shape[0]

    @pl.when(ki == 0)
    def _():
        _absorb_queries(q_ref, cos_ref[...], sin_ref[...], wuk_ref, qa_s, qr_s, H, tq, scale)
        m_s[...] = jnp.full(m_s.shape, NEG, F32)
        l_s[...] = jnp.zeros(l_s.shape, F32)
        acc_s[...] = jnp.zeros(acc_s.shape, F32)

    def update(masked):
        k = ckv_ref[...].astype(BF16)
        r = kr_ref[...].astype(BF16)
        if masked:
            keep = (lax.broadcasted_iota(jnp.int32, (tq, tk), 1)
                    <= lax.broadcasted_iota(jnp.int32, (tq, tk), 0))

        for h in range(H):
            s = _dot_nt(qa_s[h], k) + _dot_nt(qr_s[h], r)
            if masked:
                s = jnp.where(keep, s, NEG)
            m_prev = m_s[h]
            m_new = jnp.maximum(m_prev, _row_max(s))
            alpha = jnp.exp(m_prev - m_new)
            p = jnp.exp(s - m_new)
            l_s[h] = alpha * l_s[h] + _row_sum(p)
            acc_s[h] = alpha * acc_s[h] + _dot(p.astype(BF16), k)
            m_s[h] = m_new

    @pl.when(ki < qi)
    def _():
        update(False)

    @pl.when(ki == qi)
    def _():
        update(True)
        for h in range(H):
            o = acc_s[h] / l_s[h]
            y_ref[:, h * D_V:(h + 1) * D_V] = _dot(o.astype(BF16), wuv_ref[h])


def _attn_prompt(q, cos2, sin2, ckv, kr, wuk_t, wuv_h, B, L, tq):
    T = q.shape[0]
    H, _, r_kv = wuk_t.shape
    nq = L // tq
    scale = float((D_NOPE + D_ROPE) ** -0.5)
    qrow = lambda w: pl.BlockSpec((tq, w), lambda b, i, j: (b * nq + i, 0))
    krow = lambda w: pl.BlockSpec((tq, w), lambda b, i, j: (b * nq + jnp.minimum(i, j), 0))
    return pl.pallas_call(
        functools.partial(_attn_prompt_kernel, H=H, scale=scale),
        grid=(B, nq, nq),
        in_specs=[qrow(q.shape[1]), qrow(D_ROPE), qrow(D_ROPE), krow(r_kv), krow(D_ROPE),
                  _resident(wuk_t.shape), _resident(wuv_h.shape)],
        out_specs=qrow(H * D_V),
        out_shape=jax.ShapeDtypeStruct((B * L, H * D_V), F32),
        scratch_shapes=[pltpu.VMEM((H, tq, r_kv), BF16), pltpu.VMEM((H, tq, D_ROPE), BF16),
                        pltpu.VMEM((H, tq, 1), F32), pltpu.VMEM((H, tq, 1), F32),
                        pltpu.VMEM((H, tq, r_kv), F32)],
        compiler_params=_cparams("parallel", "parallel", "arbitrary"),
        name="attn_prompt",
    )(q, cos2, sin2, ckv, kr, wuk_t, wuv_h)


def _attn_sample_kernel(pt_ref, q_ref, cos_ref, sin_ref, cnew_ref, rnew_ref, wuk_ref, wuv_ref,
                        ckv_hbm, kr_hbm, y_ref,
                        kbuf, rbuf, ksem, rsem, qa_s, qr_s, m_s, l_s, acc_s,
                        *, layer, G, H, LQ, PS, NSEQ, scale):
    b, j = pl.program_id(0), pl.program_id(1)
    nb, nch = pl.num_programs(0), pl.num_programs(1)
    c = b * nch + j
    slot = c % 2

    def page_copies(bb, jj, sl):
        copies = []
        for n in range(NSEQ):
            for g in range(G):
                pg = pt_ref[bb * NSEQ + n, jj * G + g]
                copies.append(pltpu.make_async_copy(ckv_hbm.at[layer, pg], kbuf.at[sl, n, pl.ds(g * PS, PS)],
                                                    ksem.at[sl]))
                copies.append(pltpu.make_async_copy(kr_hbm.at[layer, pg], rbuf.at[sl, n, :, pl.ds(g * PS, PS)],
                                                    rsem.at[sl]))
        return copies

    @pl.when(c == 0)
    def _():
        for cp in page_copies(b, j, slot):
            cp.start()

    @pl.when(c + 1 < nb * nch)
    def _():
        last = j == nch - 1
        for cp in page_copies(jnp.where(last, b + 1, b), jnp.where(last, 0, j + 1), 1 - slot):
            cp.start()

    @pl.when(j == 0)
    def _():
        for n in range(NSEQ):
            _absorb_queries(q_ref.at[n], cos_ref[...], sin_ref[...], wuk_ref, qa_s.at[n], qr_s.at[n], H, LQ, scale)
        m_s[...] = jnp.full(m_s.shape, NEG, F32)
        l_s[...] = jnp.zeros(l_s.shape, F32)
        acc_s[...] = jnp.zeros(acc_s.shape, F32)

    for cp in page_copies(b, j, slot):
        cp.wait()

    ks = [kbuf[slot, n].astype(BF16) for n in range(NSEQ)]
    ss = [_dot_nt(qa_s[n], ks[n]) + _dot(qr_s[n], rbuf[slot, n].astype(BF16)) for n in range(NSEQ)]
    for n in range(NSEQ):
        m_prev = m_s[n]
        m_new = jnp.maximum(m_prev, _row_max(ss[n]))
        alpha = jnp.exp(m_prev - m_new)
        p = jnp.exp(ss[n] - m_new)
        l_s[n] = alpha * l_s[n] + _row_sum(p)
        acc_s[n] = alpha * acc_s[n] + _dot(p.astype(BF16), ks[n])
        m_s[n] = m_new

    @pl.when(j == nch - 1)
    def _():
        rows = H * LQ
        qpos = lax.broadcasted_iota(jnp.int32, (rows, 1), 0) % LQ
        for n in range(NSEQ):
            qa = qa_s[n].astype(F32)
            qr = qr_s[n].astype(F32)
            cols = []
            for t in range(LQ):
                st = (jnp.sum(qa * cnew_ref[n, t:t + 1, :], axis=-1, keepdims=True)
                      + jnp.sum(qr * rnew_ref[n, t:t + 1, :], axis=-1, keepdims=True))
                cols.append(jnp.where(qpos >= t, st, NEG))
            m_prev = m_s[n]
            m_new = m_prev
            for st in cols:
                m_new = jnp.maximum(m_new, st)
            alpha = jnp.exp(m_prev - m_new)
            lsum = alpha * l_s[n]
            acc = alpha * acc_s[n]
            for t, st in enumerate(cols):
                pt = jnp.exp(st - m_new)
                lsum = lsum + pt
                acc = acc + pt * cnew_ref[n, t:t + 1, :]
            o = (acc / lsum).astype(BF16)
            for h in range(H):
                y_ref[n, :, h * D_V:(h + 1) * D_V] = _dot(o[h * LQ:(h + 1) * LQ, :], wuv_ref[h])


def _attn_sample(page_table, q, cos2, sin2, cnew, rnew, wuk_t, wuv_h, cache_ckv, cache_krope_t, layer, G):
    B, LQ, nq = q.shape
    H, _, r_kv = wuk_t.shape
    PS = cache_ckv.shape[2]
    n_pages = page_table.shape[1]
    nch = n_pages // G
    NSEQ = 4 if B % 4 == 0 else 1
    R = H * LQ
    scale = float((D_NOPE + D_ROPE) ** -0.5)
    seq = lambda w: pl.BlockSpec((NSEQ, LQ, w), lambda b, j, pt: (b, 0, 0))
    const = lambda shape: pl.BlockSpec(shape, lambda b, j, pt: (0,) * len(shape), pipeline_mode=pl.Buffered(1))
    grid_spec = pltpu.PrefetchScalarGridSpec(
        num_scalar_prefetch=1,
        grid=(B // NSEQ, nch),
        in_specs=[seq(nq), const((LQ, D_ROPE)), const((LQ, D_ROPE)), seq(r_kv), seq(D_ROPE),
                  const(wuk_t.shape), const(wuv_h.shape),
                  pl.BlockSpec(memory_space=pl.ANY), pl.BlockSpec(memory_space=pl.ANY)],
        out_specs=seq(H * D_V),
        scratch_shapes=[pltpu.VMEM((2, NSEQ, G * PS, r_kv), F32), pltpu.VMEM((2, NSEQ, D_ROPE, G * PS), F32),
                        pltpu.SemaphoreType.DMA((2,)), pltpu.SemaphoreType.DMA((2,)),
                        pltpu.VMEM((NSEQ, R, r_kv), BF16), pltpu.VMEM((NSEQ, R, D_ROPE), BF16),
                        pltpu.VMEM((NSEQ, R, 1), F32), pltpu.VMEM((NSEQ, R, 1), F32),
                        pltpu.VMEM((NSEQ, R, r_kv), F32)],
    )
    return pl.pallas_call(
        functools.partial(_attn_sample_kernel, layer=layer, G=G, H=H, LQ=LQ, PS=PS, NSEQ=NSEQ, scale=scale),
        grid_spec=grid_spec,
        out_shape=jax.ShapeDtypeStruct((B, LQ, H * D_V), F32),
        compiler_params=_cparams("arbitrary", "arbitrary"),
        name="attn_sample",
    )(page_table, q, cos2, sin2, cnew, rnew, wuk_t, wuv_h, cache_ckv, cache_krope_t)


def _mem_attn_kernel(q_ref, mk_ref, mv_ref, o_ref, *, NB, LQ, MH, HD, scale):
    for n in range(NB):
        rows = slice(n * LQ, (n + 1) * LQ)
        for h in range(MH):
            cs = slice(h * HD, (h + 1) * HD)
            q = (q_ref[rows, cs] * scale).astype(BF16)
            s = _dot_nt(q, mk_ref[n, :, cs].astype(BF16))
            m = jnp.max(s, axis=-1, keepdims=True)
            p = jnp.exp(s - m)
            denom = jnp.sum(p, axis=-1, keepdims=True)
            o_ref[rows, cs] = _dot(p.astype(BF16), mv_ref[n, :, cs].astype(BF16)) / denom


def _mem_attn(q, mk, mv, row0, n_seq, LQ, NB, mem_index, T_out, name):
    C = q.shape[1]
    M = mk.shape[-2]
    HD = LANES
    MH = C // HD
    rb = NB * LQ
    assert row0 % rb == 0 and n_seq % NB == 0
    qspec = pl.BlockSpec((rb, C), lambda i: (row0 // rb + i, 0))
    lead = mk.ndim - 3
    mspec = pl.BlockSpec((None,) * lead + (NB, M, C), mem_index)
    return pl.pallas_call(
        functools.partial(_mem_attn_kernel, NB=NB, LQ=LQ, MH=MH, HD=HD, scale=float(HD ** -0.5)),
        grid=(n_seq // NB,),
        in_specs=[qspec, mspec, mspec],
        out_specs=pl.BlockSpec((rb, C), lambda i: (i, 0)),
        out_shape=jax.ShapeDtypeStruct((T_out, C), F32),
        compiler_params=_cparams("parallel"),
        name=name,
    )(q, mk, mv)


def _ffn_kernel(x_ref, g_ref, wg_ref, wu_ref, wd_ref, o_ref, xn_s):
    f = pl.program_id(1)

    @pl.when(f == 0)
    def _():
        x = x_ref[...]
        xn_s[...] = _rms(x, g_ref[...]).astype(BF16)
        o_ref[...] = x

    xn = xn_s[...]
    h = _silu(_dot(xn, wg_ref[...])) * _dot(xn, wu_ref[...])
    o_ref[...] += _dot(h.astype(BF16), wd_ref[...])


def _ffn(x, g, wg, wu, wd, tm, tf):
    T, D = x.shape
    F = wg.shape[1]
    return pl.pallas_call(
        _ffn_kernel,
        grid=(T // tm, F // tf),
        in_specs=[pl.BlockSpec((tm, D), lambda i, f: (i, 0)), _resident((1, D)),
                  pl.BlockSpec((D, tf), lambda i, f: (0, f)), pl.BlockSpec((D, tf), lambda i, f: (0, f)),
                  pl.BlockSpec((tf, D), lambda i, f: (f, 0))],
        out_specs=pl.BlockSpec((tm, D), lambda i, f: (i, 0)),
        out_shape=jax.ShapeDtypeStruct((T, D), F32),
        scratch_shapes=[pltpu.VMEM((tm, D), BF16)],
        compiler_params=_cparams("parallel", "arbitrary"),
        name="ffn_dense",
    )(x, g, wg, wu, wd)


def _slab_load(ref, s, n, S):
    return ref[pl.ds(s, n, stride=S), :]


def _slab_store(ref, s, n, S, val):
    ref[pl.ds(s, n, stride=S), :] = val


def _pack_bf16_pair(a, b):
    lo = lax.bitcast_convert_type(a.astype(BF16).astype(F32), jnp.uint32)
    hi = lax.bitcast_convert_type(b.astype(BF16).astype(F32), jnp.uint32)
    return lax.shift_right_logical(lo, jnp.uint32(16)) | (hi & jnp.uint32(0xFFFF0000))


def _unpack_bf16_pair(w):
    a = lax.bitcast_convert_type(lax.shift_left(w, jnp.uint32(16)), F32)
    b = lax.bitcast_convert_type(w & jnp.uint32(0xFFFF0000), F32)
    return a.astype(BF16), b.astype(BF16)


def _router_kernel(x_ref, g_ref, wr_ref, xn_ref, info_ref, *, E):
    xn = _rms(x_ref[...], g_ref[...])
    tm, D = xn.shape
    S = D // (2 * LANES)
    for s in range(S):
        lo = xn[:, s * LANES:(s + 1) * LANES]
        hi = xn[:, D // 2 + s * LANES:D // 2 + (s + 1) * LANES]
        _slab_store(xn_ref, s, tm, S, _pack_bf16_pair(lo, hi))
    logits = jnp.dot(xn, wr_ref[...], preferred_element_type=F32, precision=lax.Precision.HIGHEST)
    lane = lax.broadcasted_iota(jnp.int32, logits.shape, 1).astype(F32)
    valid = lane < E
    logits = jnp.where(valid, logits, NEG)
    ex = jnp.exp(logits - jnp.max(logits, axis=-1, keepdims=True))
    probs = jnp.where(valid, ex / jnp.sum(ex, axis=-1, keepdims=True), -1.0)
    big = 1e9
    p1 = jnp.max(probs, axis=-1, keepdims=True)
    i1 = jnp.min(jnp.where(probs == p1, lane, big), axis=-1, keepdims=True)
    rest = jnp.where(lane == i1, -1.0, probs)
    p2 = jnp.max(rest, axis=-1, keepdims=True)
    i2 = jnp.min(jnp.where(rest == p2, lane, big), axis=-1, keepdims=True)
    tot = p1 + p2
    w1, w2 = p1 / tot, p2 / tot
    info = jnp.where(lane == i1, w1, 0.0) + jnp.where(lane == i2, w2, 0.0)
    for k, v in enumerate((i1, i2, w1, w2)):
        info = jnp.where(lane == E + k, v, info)
    info_ref[...] = info


def _router(x, g, wr_p, E, tm):
    T, D = x.shape
    return pl.pallas_call(
        functools.partial(_router_kernel, E=E),
        grid=(T // tm,),
        in_specs=[pl.BlockSpec((tm, D), lambda i: (i, 0)), _resident((1, D)), _resident(wr_p.shape)],
        out_specs=[pl.BlockSpec((tm * (D // (2 * LANES)), LANES), lambda i: (i, 0)),
                   pl.BlockSpec((tm, LANES), lambda i: (i, 0))],
        out_shape=[jax.ShapeDtypeStruct((T * (D // (2 * LANES)), LANES), jnp.uint32),
                   jax.ShapeDtypeStruct((T, LANES), F32)],
        compiler_params=_cparams("parallel"),
        name="moe_router",
    )(x, g, wr_p)


def _moe_dispatch_kernel(zrow_ref, xn_ref, dest_hbm, xs_hbm, dsm, zbuf, sem_d, sem_z, sem_r, *, E, tm, tme, S):
    i = pl.program_id(0)
    dcopy = pltpu.make_async_copy(dest_hbm.at[pl.ds(pl.multiple_of(i * 2 * tm, 2 * tm), 2 * tm)], dsm, sem_d)
    dcopy.start()

    @pl.when(i == 0)
    def _():
        zbuf[...] = jnp.zeros(zbuf.shape, zbuf.dtype)
        n_tiles = xs_hbm.shape[0] // (tme * S)
        for wait in (False, True):
            for e in range(E):
                for row, ok in ((zrow_ref[e], zrow_ref[e] >= 0),
                                ((zrow_ref[E] + e) * tme, zrow_ref[E] + e < n_tiles)):
                    @pl.when(ok)
                    def _():
                        r0 = pl.multiple_of(row * S, tme * S)
                        cp = pltpu.make_async_copy(zbuf, xs_hbm.at[pl.ds(r0, tme * S)], sem_z)
                        cp.wait() if wait else cp.start()

    dcopy.wait()

    def row_copy(t, k):
        src = xn_ref.at[pl.ds(pl.multiple_of(t * S, S), S)]
        dst = xs_hbm.at[pl.ds(pl.multiple_of(dsm[2 * t + k] * S, S), S)]
        return pltpu.make_async_copy(src, dst, sem_r)

    def issue(t, carry):
        row_copy(t, 0).start()
        row_copy(t, 1).start()
        return carry

    def drain(t, carry):
        row_copy(t, 0).wait()
        row_copy(t, 1).wait()
        return carry

    lax.fori_loop(0, tm, issue, 0, unroll=8)
    lax.fori_loop(0, tm, drain, 0, unroll=8)


def _moe_dispatch(xn, dest, zrow, P, D, tm, tme):
    S = D // (2 * LANES)
    T = xn.shape[0] // S
    E = zrow.shape[0] - 1
    grid_spec = pltpu.PrefetchScalarGridSpec(
        num_scalar_prefetch=1,
        grid=(T // tm,),
        in_specs=[pl.BlockSpec((tm * S, LANES), lambda i, z: (i, 0)), pl.BlockSpec(memory_space=pl.ANY)],
        out_specs=pl.BlockSpec(memory_space=pl.ANY),
        scratch_shapes=[pltpu.SMEM((2 * tm,), jnp.int32), pltpu.VMEM((tme * S, LANES), xn.dtype),
                        pltpu.SemaphoreType.DMA, pltpu.SemaphoreType.DMA, pltpu.SemaphoreType.DMA],
    )
    return pl.pallas_call(
        functools.partial(_moe_dispatch_kernel, E=E, tm=tm, tme=tme, S=S),
        grid_spec=grid_spec,
        out_shape=jax.ShapeDtypeStruct((P * S, LANES), xn.dtype),
        compiler_params=_cparams("arbitrary"),
        name="moe_dispatch",
    )(zrow, xn, dest)


def _moe_ffn_kernel(te_ref, nu_ref, x_ref, wg_ref, wu_ref, wd_ref, o_ref, xn_s, acc_s, *, S):
    i, f = pl.program_id(0), pl.program_id(1)
    used = i < nu_ref[0]
    tme, D = xn_s.shape

    @pl.when(used)
    def _():
        @pl.when(f == 0)
        def _():
            for s in range(S // 2):
                lo, hi = _unpack_bf16_pair(_slab_load(x_ref, s, tme, S // 2))
                xn_s[:, s * LANES:(s + 1) * LANES] = lo
                xn_s[:, D // 2 + s * LANES:D // 2 + (s + 1) * LANES] = hi
            acc_s[...] = jnp.zeros(acc_s.shape, F32)

        xn = xn_s[...]
        h = _silu(_dot(xn, wg_ref[...])) * _dot(xn, wu_ref[...])
        acc_s[...] += _dot(h.astype(BF16), wd_ref[...])

        @pl.when(f == pl.num_programs(1) - 1)
        def _():
            for s in range(S):
                _slab_store(o_ref, s, tme, S, acc_s[:, s * LANES:(s + 1) * LANES])

    @pl.when(jnp.logical_not(used) & (f == 0))
    def _():
        o_ref[...] = jnp.zeros(o_ref.shape, F32)


def _moe_ffn(xs, tile_expert, n_used, wg, wu, wd, D, tme, tf):
    S = D // LANES
    P = xs.shape[0] // (S // 2)
    nf = wg.shape[2] // tf
    feff = lambda i, f, nu: jnp.where(i < nu[0], f, nf - 1)
    grid_spec = pltpu.PrefetchScalarGridSpec(
        num_scalar_prefetch=2,
        grid=(P // tme, nf),
        in_specs=[pl.BlockSpec((tme * S // 2, LANES), lambda i, f, te, nu: (jnp.minimum(i, nu[0] - 1), 0)),
                  pl.BlockSpec((None, D, tf), lambda i, f, te, nu: (te[i], 0, feff(i, f, nu))),
                  pl.BlockSpec((None, D, tf), lambda i, f, te, nu: (te[i], 0, feff(i, f, nu))),
                  pl.BlockSpec((None, tf, D), lambda i, f, te, nu: (te[i], feff(i, f, nu), 0))],
        out_specs=pl.BlockSpec((tme * S, LANES), lambda i, f, te, nu: (i, 0)),
        scratch_shapes=[pltpu.VMEM((tme, D), BF16), pltpu.VMEM((tme, D), F32)],
    )
    return pl.pallas_call(
        functools.partial(_moe_ffn_kernel, S=S),
        grid_spec=grid_spec,
        out_shape=jax.ShapeDtypeStruct((P * S, LANES), F32),
        compiler_params=_cparams("arbitrary", "arbitrary"),
        name="moe_ffn",
    )(tile_expert, n_used, xs, wg, wu, wd)


def _moe_combine_kernel(x_ref, info_ref, dest_hbm, ys_hbm, o_ref, dsm, buf0, buf1, sem_d, sem_r, *, E, tm, S):
    i = pl.program_id(0)
    dcopy = pltpu.make_async_copy(dest_hbm.at[pl.ds(pl.multiple_of(i * 2 * tm, 2 * tm), 2 * tm)], dsm, sem_d)
    dcopy.start()
    dcopy.wait()
    bufs = (buf0, buf1)

    def row_copy(t, k):
        src = ys_hbm.at[pl.ds(pl.multiple_of(dsm[2 * t + k] * S, S), S)]
        return pltpu.make_async_copy(src, bufs[k].at[pl.ds(pl.multiple_of(t * S, S), S)], sem_r)

    def issue(t, carry):
        row_copy(t, 0).start()
        row_copy(t, 1).start()
        return carry

    def drain(t, carry):
        row_copy(t, 0).wait()
        row_copy(t, 1).wait()
        return carry

    lax.fori_loop(0, tm, issue, 0, unroll=8)
    lax.fori_loop(0, tm, drain, 0, unroll=8)
    w1 = info_ref[:, E + 2:E + 3]
    w2 = info_ref[:, E + 3:E + 4]
    for s in range(S):
        cs = slice(s * LANES, (s + 1) * LANES)
        o_ref[:, cs] = x_ref[:, cs] + w1 * _slab_load(buf0, s, tm, S) + w2 * _slab_load(buf1, s, tm, S)


def _moe_combine(x, info, dest, ys, E, tm):
    T, D = x.shape
    S = D // LANES
    return pl.pallas_call(
        functools.partial(_moe_combine_kernel, E=E, tm=tm, S=S),
        grid=(T // tm,),
        in_specs=[pl.BlockSpec((tm, D), lambda i: (i, 0)), pl.BlockSpec((tm, LANES), lambda i: (i, 0)),
                  pl.BlockSpec(memory_space=pl.ANY), pl.BlockSpec(memory_space=pl.ANY)],
        out_specs=pl.BlockSpec((tm, D), lambda i: (i, 0)),
        out_shape=jax.ShapeDtypeStruct((T, D), F32),
        scratch_shapes=[pltpu.SMEM((2 * tm,), jnp.int32), pltpu.VMEM((tm * S, LANES), F32),
                        pltpu.VMEM((tm * S, LANES), F32), pltpu.SemaphoreType.DMA, pltpu.SemaphoreType.DMA],
        compiler_params=_cparams("arbitrary"),
        name="moe_combine",
    )(x, info, dest, ys)


def _moe_plan(info, E, tme, n_tiles):
    e_flat = info[:, E:E + 2].astype(jnp.int32).reshape(-1)
    onehot = (e_flat[:, None] == jnp.arange(E, dtype=jnp.int32)[None, :]).astype(jnp.int32)
    csum = jnp.cumsum(onehot, axis=0)
    counts = csum[-1]
    rank = jnp.take_along_axis(csum, e_flat[:, None], axis=1)[:, 0] - 1
    padded = ((counts + tme - 1) // tme) * tme
    ends = jnp.cumsum(padded)
    dest = (ends - padded)[e_flat] + rank
    n_used = (ends[-1] // tme).astype(jnp.int32)
    tile = jnp.arange(n_tiles, dtype=jnp.int32)
    tile_expert = jnp.sum((tile[:, None] * tme >= ends[None, :]).astype(jnp.int32), axis=1)
    tile_expert = jnp.minimum(tile_expert, E - 1)
    last_used = tile_expert[jnp.maximum(n_used - 1, 0)]
    tile_expert = jnp.where(tile < n_used, tile_expert, last_used)
    zrow = jnp.concatenate([jnp.where(counts > 0, ends - tme, -1), n_used.reshape(1)]).astype(jnp.int32)
    return dest.astype(jnp.int32), tile_expert.astype(jnp.int32), n_used.reshape(1), zrow


def _norm_kernel(x_ref, g_ref, o_ref):
    o_ref[...] = _rms(x_ref[...], g_ref[...])


def _final_norm(x, g, row0, n_rows, tm):
    D = x.shape[1]
    return pl.pallas_call(
        _norm_kernel,
        grid=(n_rows // tm,),
        in_specs=[pl.BlockSpec((tm, D), lambda i: (row0 // tm + i, 0)), _resident((1, D))],
        out_specs=pl.BlockSpec((tm, D), lambda i: (i, 0)),
        out_shape=jax.ShapeDtypeStruct((n_rows, D), F32),
        compiler_params=_cparams("parallel"),
        name="final_norm",
    )(x, g)


def _block_diag(w):
    n, c, d = w.shape
    eye = jnp.eye(n, dtype=w.dtype)
    return (w[:, :, None, :] * eye[:, None, :, None]).reshape(n * c, n * d)


def _pad_to(x, axis, mult):
    pad = (-x.shape[axis]) % mult
    if pad == 0:
        return x
    widths = [(0, 0)] * x.ndim
    widths[axis] = (0, pad)
    return jnp.pad(x, widths)


def kernel(x_prompt, x_sample, mem_prompt, cache_ckv, cache_krope, page_table, state_pool, state_conv, state_h, cache_mem_k, cache_mem_v, norm_mix, norm_mem, norm_memkv, norm_ffn, norm_final, w_in, w_pool, pool_scale, conv_w, conv_b, w_rg_a, b_rg_a, w_rg_x, b_rg_x, lru_lambda, q_norm, w_uq, kv_norm, w_uk, w_uv, w_out, w_mq, w_mk, w_mv, w_mo, ffn_w_gate, ffn_w_up, ffn_w_down, moe_router, moe_w_gate, moe_w_up, moe_w_down):
    B, L, D = x_prompt.shape
    BS, LS, _ = x_sample.shape
    depth = w_in.shape[0]
    d_pool = state_pool.shape[-1]
    d_rnn = state_conv.shape[-1]
    r_q = q_norm.shape[-1]
    r_kv = kv_norm.shape[-1]
    H = w_uq.shape[2]
    M, MH, MHD = cache_mem_k.shape[2:]
    E = moe_router.shape[-1]
    PS = cache_ckv.shape[2]
    n_pages = page_table.shape[1]
    past_len = n_pages * PS
    NP, NS = B * L, BS * LS
    T = NP + NS
    assert d_pool == len(POOL_WINDOWS) * LANES and MHD == LANES and w_uq.shape[3] == D_NOPE + D_ROPE

    tm = min(512, NS)
    assert NP % tm == 0 and NS % tm == 0
    tq = min(512, L)
    TL = min(512, L)
    G = max(1, min(16, n_pages // 2))
    assert n_pages % G == 0
    offs = (0, d_pool, d_pool + d_rnn, d_pool + 2 * d_rnn, d_pool + 2 * d_rnn + r_q,
            d_pool + 2 * d_rnn + r_q + r_kv)
    n_in = offs[-1] + D_ROPE
    offs = offs + (offs[-1] + LANES,)

    half = D_ROPE // 2
    freqs = ROPE_BASE ** (-jnp.arange(half, dtype=F32) / half)
    pos_p = jnp.arange(L, dtype=jnp.int32)
    pos_s = past_len + jnp.arange(LS, dtype=jnp.int32)

    def tables(pos):
        ang = pos.astype(F32)[:, None] * freqs[None, :]
        c, s = jnp.cos(ang), jnp.sin(ang)
        return jnp.concatenate([c, c], -1), jnp.concatenate([-s, s], -1)

    cos_p, sin_p = tables(pos_p)
    cos_s, sin_s = tables(pos_s)
    cos_all = jnp.concatenate([jnp.tile(cos_p, (B, 1)), jnp.tile(cos_s, (BS, 1))], 0)
    sin_all = jnp.concatenate([jnp.tile(sin_p, (B, 1)), jnp.tile(sin_s, (BS, 1))], 0)

    x = jnp.concatenate([x_prompt.reshape(NP, D), x_sample.reshape(NS, D)], 0)
    mem2 = mem_prompt.reshape(B * M, D)
    krope_t = jnp.swapaxes(cache_krope, 2, 3)
    cmk = cache_mem_k.reshape(depth, BS, M, MH * MHD)
    cmv = cache_mem_v.reshape(depth, BS, M, MH * MHD)
    row = lambda v: v.reshape(1, -1)

    outs = {k: [] for k in ('pc', 'pk', 'pp', 'pcv', 'ph', 'pmk', 'pmv', 'sc', 'sk', 'sp', 'scv', 'sh')}
    for l in range(depth):
        w_in_p = _pad_to(w_in[l], 1, LANES)
        w_in_p = jnp.pad(w_in_p, ((0, 0), (0, offs[-1] - w_in_p.shape[1]))).astype(BF16)
        wuq = w_uq[l]
        w_uq_p = jnp.concatenate([wuq[:, :, :D_NOPE].reshape(r_q, H * D_NOPE),
                                  wuq[:, :, D_NOPE:].reshape(r_q, H * D_ROPE)], 1).astype(BF16)
        wuk_t = jnp.transpose(w_uk[l], (1, 2, 0)).astype(BF16)
        wuv_h = jnp.transpose(w_uv[l], (1, 0, 2)).astype(BF16)
        lw = {
            'w_pool': w_pool[l].astype(BF16), 'pool_scale': row(pool_scale[l]),
            'conv_w': conv_w[l], 'conv_b': row(conv_b[l]),
            'w_rg_a_bd': _block_diag(w_rg_a[l]).astype(BF16), 'b_rg_a': row(b_rg_a[l]),
            'w_rg_x_bd': _block_diag(w_rg_x[l]).astype(BF16), 'b_rg_x': row(b_rg_x[l]),
            'lru_lambda': row(lru_lambda[l]),
        }

        u, xb, gb, q, ckv, kr = _in_proj(x, row(norm_mix[l]), w_in_p, row(q_norm[l]), w_uq_p,
                                         row(kv_norm[l]), cos_all, sin_all, offs, tm)

        y_pool, y_rnn, pool_new, conv_new, h_new = _seqmix_prompt(u, xb, gb, lw, B, L, TL)
        tmaj = lambda a: jnp.transpose(a[NP:].reshape(BS, LS, -1), (1, 0, 2))
        u_s, xb_s, gb_s = tmaj(u), tmaj(xb), tmaj(gb)
        yp_s, yr_s, h_s = _seqmix_sample(u_s, xb_s, gb_s, jnp.transpose(state_pool[l], (1, 0, 2)),
                                         jnp.transpose(state_conv[l], (1, 0, 2)), state_h[l], lw, past_len)
        bmaj = lambda a: jnp.transpose(a, (1, 0, 2)).reshape(NS, -1)

        y_mla = _attn_prompt(q, cos_all, sin_all, ckv, kr, wuk_t, wuv_h, B, L, tq)
        s_ckv = ckv[NP:].reshape(BS, LS, r_kv)
        s_kr = kr[NP:].reshape(BS, LS, D_ROPE)
        y_mla_s = _attn_sample(page_table, q[NP:].reshape(BS, LS, -1), cos_s, sin_s, s_ckv, s_kr,
                               wuk_t, wuv_h, cache_ckv, krope_t, l, G)
        x = _mm_res([(y_pool, bmaj(yp_s)), (y_rnn, bmaj(yr_s)), (y_mla, y_mla_s.reshape(NS, -1))],
                    w_out[l].astype(BF16), x, tm, "out_proj")

        w_mkv = jnp.concatenate([w_mk[l].reshape(D, MH * MHD), w_mv[l].reshape(D, MH * MHD)], 1).astype(BF16)
        mk_p, mv_p = _norm_mm(mem2, row(norm_memkv[l]), w_mkv,
                              ((0, MH * MHD), (MH * MHD, 2 * MH * MHD)), min(512, B * M), "mem_kv")
        (qm,) = _norm_mm(x, row(norm_mem[l]), w_mq[l].reshape(D, MH * MHD).astype(BF16),
                         ((0, MH * MHD),), tm, "mem_q")
        lq_p = min(512, L)
        o_mem = _mem_attn(qm, mk_p.reshape(B, M, -1), mv_p.reshape(B, M, -1), 0, NP // lq_p, lq_p, 1,
                          lambda i: (i // (L // lq_p), 0, 0), NP, "mem_attn_prompt")
        nb_s = 8 if BS % 8 == 0 else 1
        o_mem_s = _mem_attn(qm, cmk, cmv, NP, BS, LS, nb_s, lambda i: (l, i, 0, 0), NS, "mem_attn_sample")
        x = _mm_res([(o_mem, o_mem_s)], w_mo[l].reshape(MH * MHD, D).astype(BF16), x, tm, "mem_out")

        j = l // 2
        if l % 2 == 0:
            x = _ffn(x, row(norm_ffn[l]), ffn_w_gate[j].astype(BF16), ffn_w_up[j].astype(BF16),
                     ffn_w_down[j].astype(BF16), tm, 512)
        else:
            wr_p = _pad_to(moe_router[j], 1, LANES)
            xn, info = _router(x, row(norm_ffn[l]), wr_p, E, tm)
            tf = 512
            wg = _pad_to(moe_w_gate[j], 2, tf).astype(BF16)
            wu = _pad_to(moe_w_up[j], 2, tf).astype(BF16)
            wd = _pad_to(moe_w_down[j], 1, tf).astype(BF16)
            tme = 512 if T >= 4096 else 128
            n_tiles = (2 * T + E * (tme - 1) + tme - 1) // tme
            dest, tile_expert, n_used, zrow = _moe_plan(info, E, tme, n_tiles)
            xs = _moe_dispatch(xn, dest, zrow, n_tiles * tme, D, tm, tme)
            ys = _moe_ffn(xs, tile_expert, n_used, wg, wu, wd, D, tme, tf)
            x = _moe_combine(x, info, dest, ys, E, tm)

        outs['pc'].append(ckv[:NP].reshape(B, L, r_kv))
        outs['pk'].append(kr[:NP].reshape(B, L, D_ROPE))
        outs['pp'].append(pool_new)
        outs['pcv'].append(conv_new)
        outs['ph'].append(h_new.reshape(B, d_rnn))
        outs['pmk'].append(mk_p.reshape(B, M, MH, MHD))
        outs['pmv'].append(mv_p.reshape(B, M, MH, MHD))
        outs['sc'].append(s_ckv)
        outs['sk'].append(s_kr)
        u_sb = u[NP:].reshape(BS, LS, d_pool)
        xb_sb = xb[NP:].reshape(BS, LS, d_rnn)
        outs['sp'].append(jnp.concatenate([state_pool[l], u_sb], 1)[:, -POOL_STATE:])
        outs['scv'].append(jnp.concatenate([state_conv[l], xb_sb], 1)[:, -(CONV_W - 1):])
        outs['sh'].append(h_s)

    y_prompt = _final_norm(x, row(norm_final), 0, NP, tm).reshape(B, L, D)
    y_sample = _final_norm(x, row(norm_final), NP, NS, tm).reshape(BS, LS, D)
    st = lambda k: jnp.stack(outs[k])
    return (y_prompt, y_sample, st('pc'), st('pk'), st('pp'), st('pcv'), st('ph'), st('pmk'), st('pmv'),
            st('sc'), st('sk'), st('sp'), st('scv'), st('sh'))
```

```python
import functools

import jax
import jax.numpy as jnp
from jax import lax
from jax.experimental import pallas as pl
from jax.experimental.pallas import tpu as pltpu

F32 = jnp.float32
BF16 = jnp.bfloat16

EPS = 1e-6
POOL_WINDOWS = (2, 4, 8, 16)
POOL_STATE = max(POOL_WINDOWS) - 1
CONV_W = 4
LRU_C = 8.0
ROPE_BASE = 10000.0
D_NOPE = 128
D_ROPE = 64
D_V = 128
LANES = 128
VMEM_LIMIT = 56 * 1024 * 1024
NEG = -1e30


def _cparams(*sem):
    return pltpu.CompilerParams(dimension_semantics=sem, vmem_limit_bytes=VMEM_LIMIT)


def _resident(shape):
    nd = len(shape)
    return pl.BlockSpec(shape, lambda *_: (0,) * nd, pipeline_mode=pl.Buffered(1))


def _rms(x, g):
    ms = jnp.mean(x * x, axis=-1, keepdims=True)
    return x * lax.rsqrt(ms + EPS) * g


def _dot(a, b):
    return jnp.dot(a, b, preferred_element_type=F32)


def _dot_nt(a, b):
    return lax.dot_general(a, b, (((1,), (1,)), ((), ())), preferred_element_type=F32)


def _rope64(x, cos2, sin2):
    half = x.shape[-1] // 2
    rot = jnp.concatenate([x[:, half:], x[:, :half]], axis=-1)
    return x * cos2 + rot * sin2


def _silu(x):
    return x * jax.nn.sigmoid(x)


def _gelu_tanh(x):
    return 0.5 * x * (1.0 + jnp.tanh(0.7978845608028654 * (x + 0.044715 * (x * x * x))))


def _softplus(x):
    return jnp.maximum(x, 0.0) + jnp.log1p(jnp.exp(-jnp.abs(x)))


def _in_proj_kernel(x_ref, g_ref, w_ref, qn_ref, wuq_ref, kvn_ref, cos_ref, sin_ref,
                    u_ref, xb_ref, gb_ref, q_ref, ckv_ref, kr_ref, *, offs):
    o_pool, o_rx, o_rg, o_cq, o_ckv, o_kr, o_end = offs
    xn = _rms(x_ref[...], g_ref[...]).astype(BF16)

    def mm(a, b):
        return _dot(xn, w_ref[:, a:b])

    u_ref[...] = mm(o_pool, o_rx)
    xb_ref[...] = mm(o_rx, o_rg)
    gb_ref[...] = mm(o_rg, o_cq)
    cq = mm(o_cq, o_ckv)
    q_ref[...] = _dot(_rms(cq, qn_ref[...]).astype(BF16), wuq_ref[...])
    kv = mm(o_ckv, o_end)
    r_kv = o_kr - o_ckv
    ckv_ref[...] = _rms(kv[:, :r_kv], kvn_ref[...])
    kr_ref[...] = _rope64(kv[:, r_kv:r_kv + D_ROPE], cos_ref[...], sin_ref[...])


def _in_proj(x, g, w_in_p, q_norm, w_uq_p, kv_norm, cos2, sin2, offs, tm):
    T, D = x.shape
    n_in = w_in_p.shape[1]
    d_pool = offs[1] - offs[0]
    d_rnn = offs[2] - offs[1]
    r_q = offs[4] - offs[3]
    r_kv = offs[5] - offs[4]
    nq = w_uq_p.shape[1]
    row = lambda w: pl.BlockSpec((tm, w), lambda i: (i, 0))
    return pl.pallas_call(
        functools.partial(_in_proj_kernel, offs=offs),
        grid=(T // tm,),
        in_specs=[row(D), _resident((1, D)), _resident((D, n_in)), _resident((1, r_q)),
                  _resident((r_q, nq)), _resident((1, r_kv)), row(D_ROPE), row(D_ROPE)],
        out_specs=[row(d_pool), row(d_rnn), row(d_rnn), row(nq), row(r_kv), row(D_ROPE)],
        out_shape=[jax.ShapeDtypeStruct((T, w), F32) for w in (d_pool, d_rnn, d_rnn, nq, r_kv, D_ROPE)],
        compiler_params=_cparams("parallel"),
        name="in_proj",
    )(x, g, w_in_p, q_norm, w_uq_p, kv_norm, cos2, sin2)


def _norm_mm_kernel(x_ref, g_ref, w_ref, *o_refs, splits):
    xn = _rms(x_ref[...], g_ref[...]).astype(BF16)
    for o_ref, (a, b) in zip(o_refs, splits):
        o_ref[...] = _dot(xn, w_ref[:, a:b])


def _norm_mm(x, g, w, splits, tm, name):
    T, K = x.shape
    return pl.pallas_call(
        functools.partial(_norm_mm_kernel, splits=splits),
        grid=(T // tm,),
        in_specs=[pl.BlockSpec((tm, K), lambda i: (i, 0)), _resident((1, K)), _resident(w.shape)],
        out_specs=[pl.BlockSpec((tm, b - a), lambda i: (i, 0)) for a, b in splits],
        out_shape=[jax.ShapeDtypeStruct((T, b - a), F32) for a, b in splits],
        compiler_params=_cparams("parallel"),
        name=name,
    )(x, g, w)


def _mm_res_kernel(*refs, n, n_ptiles):
    ap_refs, as_refs, w_refs = refs[:n], refs[n:2 * n], refs[2 * n:3 * n]
    res_ref, o_ref = refs[3 * n], refs[3 * n + 1]
    i = pl.program_id(0)

    def body(a_refs):
        acc = res_ref[...]
        for a_ref, w_ref in zip(a_refs, w_refs):
            acc = acc + _dot(a_ref[...].astype(BF16), w_ref[...])
        o_ref[...] = acc

    @pl.when(i < n_ptiles)
    def _():
        body(ap_refs)

    @pl.when(i >= n_ptiles)
    def _():
        body(as_refs)


def _mm_res(a_pairs, w, res, tm, name):
    T, N = res.shape
    n = len(a_pairs)
    n_ptiles = a_pairs[0][0].shape[0] // tm
    n_stiles = a_pairs[0][1].shape[0] // tm
    assert n_ptiles + n_stiles == T // tm
    p_specs, s_specs, w_specs, r0 = [], [], [], 0
    for a_p, a_s in a_pairs:
        k = a_p.shape[1]
        assert r0 % k == 0 and a_p.shape[0] == n_ptiles * tm and a_s.shape == (n_stiles * tm, k)
        p_specs.append(pl.BlockSpec((tm, k), lambda i: (jnp.minimum(i, n_ptiles - 1), 0)))
        s_specs.append(pl.BlockSpec((tm, k), lambda i: (jnp.maximum(i - n_ptiles, 0), 0)))
        w_specs.append(pl.BlockSpec((k, N), functools.partial(lambda blk, i: (blk, 0), r0 // k),
                                    pipeline_mode=pl.Buffered(1)))
        r0 += k
    return pl.pallas_call(
        functools.partial(_mm_res_kernel, n=n, n_ptiles=n_ptiles),
        grid=(T // tm,),
        in_specs=p_specs + s_specs + w_specs + [pl.BlockSpec((tm, N), lambda i: (i, 0))],
        out_specs=pl.BlockSpec((tm, N), lambda i: (i, 0)),
        out_shape=jax.ShapeDtypeStruct((T, N), F32),
        compiler_params=_cparams("parallel"),
        name=name,
    )(*[a for a, _ in a_pairs], *[a for _, a in a_pairs], *([w] * n), res)


def _lru_coeffs(xc, wa_ref, ba_ref, wx_ref, bx_ref, sp):
    xcb = xc.astype(BF16)
    r = jax.nn.sigmoid(_dot(xcb, wa_ref[...]) + ba_ref[...])
    i = jax.nn.sigmoid(_dot(xcb, wx_ref[...]) + bx_ref[...])
    log_a = -LRU_C * r * sp
    a = jnp.exp(log_a)
    b = jnp.sqrt(jnp.tanh(-log_a) * (1.0 + a * a)) * (i * xc)
    return a, b


def _seqmix_prompt_kernel(u_ref, xb_ref, gb_ref, wpool_ref, pscale_ref, convw_ref, convb_ref,
                          wa_ref, ba_ref, wx_ref, bx_ref, lam_ref,
                          ypool_ref, yrnn_ref, pool_new_ref, conv_new_ref, h_new_ref,
                          ext_ref, extc_ref, a_ref, b_ref, hc_ref, *, TL, CH):
    l = pl.program_id(1)
    PH, CHist = 16, 8

    @pl.when(l == 0)
    def _():
        ext_ref[0:PH, :] = jnp.zeros((PH, ext_ref.shape[1]), F32)
        extc_ref[0:CHist, :] = jnp.zeros((CHist, extc_ref.shape[1]), F32)
        hc_ref[...] = jnp.zeros(hc_ref.shape, F32)

    @pl.when(l > 0)
    def _():
        ext_ref[0:PH, :] = ext_ref[TL:TL + PH, :]
        extc_ref[0:CHist, :] = extc_ref[TL:TL + CHist, :]

    ext_ref[PH:PH + TL, :] = u_ref[...]
    extc_ref[CHist:CHist + TL, :] = xb_ref[...]

    sp = _softplus(-lam_ref[...])
    gw = LANES
    for c in range(TL // CH):
        r0 = c * CH
        t = l * TL + r0 + lax.broadcasted_iota(jnp.int32, (CH, 1), 0)
        for g, w in enumerate(POOL_WINDOWS):
            cs = slice(g * gw, (g + 1) * gw)
            acc = ext_ref[PH + r0:PH + r0 + CH, cs]
            for k in range(1, w):
                acc = acc + ext_ref[PH + r0 - k:PH + r0 - k + CH, cs]
            cnt = jnp.minimum(t + 1, w).astype(F32)
            p = acc / cnt - u_ref[r0:r0 + CH, cs]
            ypool_ref[r0:r0 + CH, cs] = _dot(p.astype(BF16), wpool_ref[g]) * pscale_ref[:, cs]
        xc = convb_ref[...] + convw_ref[0:1, :] * extc_ref[CHist + r0 - (CONV_W - 1):CHist + r0 - (CONV_W - 1) + CH, :]
        for k in range(1, CONV_W):
            o = CHist + r0 - (CONV_W - 1) + k
            xc = xc + convw_ref[k:k + 1, :] * extc_ref[o:o + CH, :]
        a, b = _lru_coeffs(xc, wa_ref, ba_ref, wx_ref, bx_ref, sp)
        a_ref[r0:r0 + CH, :] = a
        b_ref[r0:r0 + CH, :] = b

    def step(i, h):
        o = pl.multiple_of(i * 8, 8)
        a8 = a_ref[pl.ds(o, 8), :]
        b8 = b_ref[pl.ds(o, 8), :]
        rows = []
        for r in range(8):
            h = a8[r:r + 1, :] * h + b8[r:r + 1, :]
            rows.append(h)
        b_ref[pl.ds(o, 8), :] = jnp.concatenate(rows, axis=0)
        return h

    h_last = lax.fori_loop(0, TL // 8, step, hc_ref[0:1, :])
    hc_ref[0:1, :] = h_last
    yrnn_ref[...] = _gelu_tanh(gb_ref[...]) * b_ref[...]
    pool_new_ref[...] = ext_ref[PH + TL - POOL_STATE:PH + TL, :]
    conv_new_ref[...] = extc_ref[CHist + TL - (CONV_W - 1):CHist + TL, :]
    h_new_ref[...] = h_last


def _seqmix_prompt(u, xb, gb, lw, B, L, TL):
    T, C = u.shape
    nl = L // TL
    CH = min(128, TL)
    seq = lambda w: pl.BlockSpec((TL, w), lambda b, l: (b * nl + l, 0))
    state = lambda r: pl.BlockSpec((None, r, C), lambda b, l: (b, 0, 0))
    return pl.pallas_call(
        functools.partial(_seqmix_prompt_kernel, TL=TL, CH=CH),
        grid=(B, nl),
        in_specs=[seq(C), seq(C), seq(C), _resident(lw['w_pool'].shape), _resident((1, C)),
                  _resident((CONV_W, C)), _resident((1, C)), _resident((C, C)), _resident((1, C)),
                  _resident((C, C)), _resident((1, C)), _resident((1, C))],
        out_specs=[seq(C), seq(C), state(POOL_STATE), state(CONV_W - 1), state(1)],
        out_shape=[jax.ShapeDtypeStruct((B * L, C), F32), jax.ShapeDtypeStruct((B * L, C), F32),
                   jax.ShapeDtypeStruct((B, POOL_STATE, C), F32),
                   jax.ShapeDtypeStruct((B, CONV_W - 1, C), F32),
                   jax.ShapeDtypeStruct((B, 1, C), F32)],
        scratch_shapes=[pltpu.VMEM((16 + TL, C), F32), pltpu.VMEM((8 + TL, C), F32),
                        pltpu.VMEM((TL, C), F32), pltpu.VMEM((TL, C), F32), pltpu.VMEM((8, C), F32)],
        compiler_params=_cparams("parallel", "arbitrary"),
        name="seqmix_prompt",
    )(u, xb, gb, lw['w_pool'], lw['pool_scale'], lw['conv_w'], lw['conv_b'],
      lw['w_rg_a_bd'], lw['b_rg_a'], lw['w_rg_x_bd'], lw['b_rg_x'], lw['lru_lambda'])


def _seqmix_sample_kernel(u_ref, xb_ref, gb_ref, sp_ref, sc_ref, h0_ref, wpool_ref, pscale_ref,
                          convw_ref, convb_ref, wa_ref, ba_ref, wx_ref, bx_ref, lam_ref,
                          ypool_ref, yrnn_ref, h_new_ref, *, LQ, pos0):
    gw = LANES
    sp = _softplus(-lam_ref[...])

    def ext(j, cs):
        return sp_ref[j, :, cs] if j < POOL_STATE else u_ref[j - POOL_STATE, :, cs]

    def extc(j):
        return sc_ref[j] if j < CONV_W - 1 else xb_ref[j - (CONV_W - 1)]

    h = h0_ref[...]
    for t in range(LQ):
        for g, w in enumerate(POOL_WINDOWS):
            cs = slice(g * gw, (g + 1) * gw)
            e = POOL_STATE + t
            acc = ext(e, cs)
            for k in range(1, w):
                acc = acc + ext(e - k, cs)
            cnt = float(min(pos0 + t + 1, w))
            p = acc / cnt - u_ref[t, :, cs]
            ypool_ref[t, :, cs] = _dot(p.astype(BF16), wpool_ref[g]) * pscale_ref[:, cs]
        xc = convb_ref[...] + convw_ref[0:1, :] * extc(t)
        for k in range(1, CONV_W):
            xc = xc + convw_ref[k:k + 1, :] * extc(t + k)
        a, b = _lru_coeffs(xc, wa_ref, ba_ref, wx_ref, bx_ref, sp)
        h = a * h + b
        yrnn_ref[t] = _gelu_tanh(gb_ref[t]) * h
    h_new_ref[...] = h


def _seqmix_sample(u, xb, gb, st_pool, st_conv, h0, lw, pos0):
    LQ, B, C = u.shape
    BT = B // 2 if B % 16 == 0 else B
    tb = lambda r: pl.BlockSpec((r, BT, C), lambda i: (0, i, 0))
    return pl.pallas_call(
        functools.partial(_seqmix_sample_kernel, LQ=LQ, pos0=pos0),
        grid=(B // BT,),
        in_specs=[tb(LQ), tb(LQ), tb(LQ), tb(POOL_STATE), tb(CONV_W - 1),
                  pl.BlockSpec((BT, C), lambda i: (i, 0)),
                  _resident(lw['w_pool'].shape), _resident((1, C)), _resident((CONV_W, C)),
                  _resident((1, C)), _resident((C, C)), _resident((1, C)), _resident((C, C)),
                  _resident((1, C)), _resident((1, C))],
        out_specs=[tb(LQ), tb(LQ), pl.BlockSpec((BT, C), lambda i: (i, 0))],
        out_shape=[jax.ShapeDtypeStruct((LQ, B, C), F32), jax.ShapeDtypeStruct((LQ, B, C), F32),
                   jax.ShapeDtypeStruct((B, C), F32)],
        compiler_params=_cparams("parallel"),
        name="seqmix_sample",
    )(u, xb, gb, st_pool, st_conv, h0, lw['w_pool'], lw['pool_scale'], lw['conv_w'], lw['conv_b'],
      lw['w_rg_a_bd'], lw['b_rg_a'], lw['w_rg_x_bd'], lw['b_rg_x'], lw['lru_lambda'])


def _lane_fold(x, op):
    out = x[:, :LANES]
    for c in range(1, x.shape[1] // LANES):
        out = op(out, x[:, c * LANES:(c + 1) * LANES])
    return out


def _row_max(x):
    return jnp.max(_lane_fold(x, jnp.maximum), axis=-1, keepdims=True)


def _row_sum(x):
    return jnp.sum(_lane_fold(x, jnp.add), axis=-1, keepdims=True)


def _absorb_queries(q_ref, cos, sin, wuk_ref, qa_s, qr_s, H, rows, scale):
    n_nope = H * D_NOPE
    for h in range(H):
        qn = q_ref[:, h * D_NOPE:(h + 1) * D_NOPE].astype(BF16)
        qa = _dot(qn, wuk_ref[h]) * scale
        qr = _rope64(q_ref[:, n_nope + h * D_ROPE:n_nope + (h + 1) * D_ROPE], cos, sin) * scale
        if qa_s.ndim == 3:
            qa_s[h] = qa.astype(BF16)
            qr_s[h] = qr.astype(BF16)
        else:
            qa_s[h * rows:(h + 1) * rows, :] = qa.astype(BF16)
            qr_s[h * rows:(h + 1) * rows, :] = qr.astype(BF16)


def _attn_prompt_kernel(q_ref, cos_ref, sin_ref, ckv_ref, kr_ref, wuk_ref, wuv_ref, y_ref,
                        qa_s, qr_s, m_s, l_s, acc_s, *, H, scale):
    qi, ki = pl.program_id(1), pl.program_id(2)
    tq, tk = q_ref.shape[0], ckv_ref.shape[0]

    @pl.when(ki == 0)
    def _():
        _absorb_queries(q_ref, cos_ref[...], sin_ref[...], wuk_ref, qa_s, qr_s, H, tq, scale)
        m_s[...] = jnp.full(m_s.shape, NEG, F32)
        l_s[...] = jnp.zeros(l_s.shape, F32)
        acc_s[...] = jnp.zeros(acc_s.shape, F32)

    def update(masked):
        k = ckv_ref[...].astype(BF16)
        r = kr_ref[...].astype(BF16)
        if masked:
            keep = (lax.broadcasted_iota(jnp.int32, (tq, tk), 1)
                    <= lax.broadcasted_iota(jnp.int32, (tq, tk), 0))

        for h in range(H):
            s = _dot_nt(qa_s[h], k) + _dot_nt(qr_s[h], r)
            if masked:
                s = jnp.where(keep, s, NEG)
            m_prev = m_s[h]
            m_new = jnp.maximum(m_prev, jnp.broadcast_to(_row_max(s), m_prev.shape))
            alpha = jnp.exp(m_prev - m_new)
            ps = [jnp.exp(s[:, c * LANES:(c + 1) * LANES] - m_new) for c in range(tk // LANES)]
            psum = ps[0]
            for pc in ps[1:]:
                psum = psum + pc
            l_s[h] = alpha * l_s[h] + psum
            pv = _dot(jnp.concatenate(ps, axis=1).astype(BF16), k)
            acc = acc_s[h]
            acc_s[h] = jnp.concatenate(
                [alpha * acc[:, c * LANES:(c + 1) * LANES] for c in range(acc.shape[1] // LANES)], axis=1) + pv
            m_s[h] = m_new

    @pl.when(ki < qi)
    def _():
        update(False)

    @pl.when(ki == qi)
    def _():
        update(True)
        for h in range(H):
            o = acc_s[h] / jnp.sum(l_s[h], axis=-1, keepdims=True)
            y_ref[:, h * D_V:(h + 1) * D_V] = _dot(o.astype(BF16), wuv_ref[h])


def _attn_prompt(q, cos2, sin2, ckv, kr, wuk_t, wuv_h, B, L, tq):
    T = q.shape[0]
    H, _, r_kv = wuk_t.shape
    nq = L // tq
    scale = float((D_NOPE + D_ROPE) ** -0.5)
    qrow = lambda w: pl.BlockSpec((tq, w), lambda b, i, j: (b * nq + i, 0))
    krow = lambda w: pl.BlockSpec((tq, w), lambda b, i, j: (b * nq + jnp.minimum(i, j), 0))
    return pl.pallas_call(
        functools.partial(_attn_prompt_kernel, H=H, scale=scale),
        grid=(B, nq, nq),
        in_specs=[qrow(q.shape[1]), qrow(D_ROPE), qrow(D_ROPE), krow(r_kv), krow(D_ROPE),
                  _resident(wuk_t.shape), _resident(wuv_h.shape)],
        out_specs=qrow(H * D_V),
        out_shape=jax.ShapeDtypeStruct((B * L, H * D_V), F32),
        scratch_shapes=[pltpu.VMEM((H, tq, r_kv), BF16), pltpu.VMEM((H, tq, D_ROPE), BF16),
                        pltpu.VMEM((H, tq, LANES), F32), pltpu.VMEM((H, tq, LANES), F32),
                        pltpu.VMEM((H, tq, r_kv), F32)],
        compiler_params=_cparams("parallel", "parallel", "arbitrary"),
        name="attn_prompt",
    )(q, cos2, sin2, ckv, kr, wuk_t, wuv_h)


def _attn_sample_kernel(pt_ref, q_ref, cos_ref, sin_ref, cnew_ref, rnew_ref, wuk_ref, wuv_ref,
                        ckv_hbm, kr_hbm, y_ref,
                        kbuf, rbuf, ksem, rsem, qa_s, qr_s, m_s, l_s, acc_s,
                        *, layer, G, H, LQ, PS, NSEQ, scale):
    b, j = pl.program_id(0), pl.program_id(1)
    nb, nch = pl.num_programs(0), pl.num_programs(1)
    c = b * nch + j
    slot = c % 2

    def page_copies(bb, jj, sl):
        copies = []
        for n in range(NSEQ):
            for g in range(G):
                pg = pt_ref[bb * NSEQ + n, jj * G + g]
                copies.append(pltpu.make_async_copy(ckv_hbm.at[layer, pg], kbuf.at[sl, n, pl.ds(g * PS, PS)],
                                                    ksem.at[sl]))
                copies.append(pltpu.make_async_copy(kr_hbm.at[layer, pg], rbuf.at[sl, n, :, pl.ds(g * PS, PS)],
                                                    rsem.at[sl]))
        return copies

    @pl.when(c == 0)
    def _():
        for cp in page_copies(b, j, slot):
            cp.start()

    @pl.when(c + 1 < nb * nch)
    def _():
        last = j == nch - 1
        for cp in page_copies(jnp.where(last, b + 1, b), jnp.where(last, 0, j + 1), 1 - slot):
            cp.start()

    @pl.when(j == 0)
    def _():
        for n in range(NSEQ):
            _absorb_queries(q_ref.at[n], cos_ref[...], sin_ref[...], wuk_ref, qa_s.at[n], qr_s.at[n], H, LQ, scale)
        m_s[...] = jnp.full(m_s.shape, NEG, F32)
        l_s[...] = jnp.zeros(l_s.shape, F32)
        acc_s[...] = jnp.zeros(acc_s.shape, F32)

    for cp in page_copies(b, j, slot):
        cp.wait()

    ks = [kbuf[slot, n].astype(BF16) for n in range(NSEQ)]
    ss = [_dot_nt(qa_s[n], ks[n]) + _dot(qr_s[n], rbuf[slot, n].astype(BF16)) for n in range(NSEQ)]
    for n in range(NSEQ):
        m_prev = m_s[n]
        m_new = jnp.maximum(m_prev, _row_max(ss[n]))
        alpha = jnp.exp(m_prev - m_new)
        p = jnp.exp(ss[n] - m_new)
        l_s[n] = alpha * l_s[n] + _row_sum(p)
        acc_s[n] = alpha * acc_s[n] + _dot(p.astype(BF16), ks[n])
        m_s[n] = m_new

    @pl.when(j == nch - 1)
    def _():
        rows = H * LQ
        qpos = lax.broadcasted_iota(jnp.int32, (rows, 1), 0) % LQ
        for n in range(NSEQ):
            qa = qa_s[n].astype(F32)
            qr = qr_s[n].astype(F32)
            cols = []
            for t in range(LQ):
                st = (jnp.sum(qa * cnew_ref[n, t:t + 1, :], axis=-1, keepdims=True)
                      + jnp.sum(qr * rnew_ref[n, t:t + 1, :], axis=-1, keepdims=True))
                cols.append(jnp.where(qpos >= t, st, NEG))
            m_prev = m_s[n]
            m_new = m_prev
            for st in cols:
                m_new = jnp.maximum(m_new, st)
            alpha = jnp.exp(m_prev - m_new)
            lsum = alpha * l_s[n]
            acc = alpha * acc_s[n]
            for t, st in enumerate(cols):
                pt = jnp.exp(st - m_new)
                lsum = lsum + pt
                acc = acc + pt * cnew_ref[n, t:t + 1, :]
            o = (acc / lsum).astype(BF16)
            for h in range(H):
                y_ref[n, :, h * D_V:(h + 1) * D_V] = _dot(o[h * LQ:(h + 1) * LQ, :], wuv_ref[h])


def _attn_sample(page_table, q, cos2, sin2, cnew, rnew, wuk_t, wuv_h, cache_ckv, cache_krope_t, layer, G):
    B, LQ, nq = q.shape
    H, _, r_kv = wuk_t.shape
    PS = cache_ckv.shape[2]
    n_pages = page_table.shape[1]
    nch = n_pages // G
    NSEQ = 4 if B % 4 == 0 else 1
    R = H * LQ
    scale = float((D_NOPE + D_ROPE) ** -0.5)
    seq = lambda w: pl.BlockSpec((NSEQ, LQ, w), lambda b, j, pt: (b, 0, 0))
    const = lambda shape: pl.BlockSpec(shape, lambda b, j, pt: (0,) * len(shape), pipeline_mode=pl.Buffered(1))
    grid_spec = pltpu.PrefetchScalarGridSpec(
        num_scalar_prefetch=1,
        grid=(B // NSEQ, nch),
        in_specs=[seq(nq), const((LQ, D_ROPE)), const((LQ, D_ROPE)), seq(r_kv), seq(D_ROPE),
                  const(wuk_t.shape), const(wuv_h.shape),
                  pl.BlockSpec(memory_space=pl.ANY), pl.BlockSpec(memory_space=pl.ANY)],
        out_specs=seq(H * D_V),
        scratch_shapes=[pltpu.VMEM((2, NSEQ, G * PS, r_kv), F32), pltpu.VMEM((2, NSEQ, D_ROPE, G * PS), F32),
                        pltpu.SemaphoreType.DMA((2,)), pltpu.SemaphoreType.DMA((2,)),
                        pltpu.VMEM((NSEQ, R, r_kv), BF16), pltpu.VMEM((NSEQ, R, D_ROPE), BF16),
                        pltpu.VMEM((NSEQ, R, 1), F32), pltpu.VMEM((NSEQ, R, 1), F32),
                        pltpu.VMEM((NSEQ, R, r_kv), F32)],
    )
    return pl.pallas_call(
        functools.partial(_attn_sample_kernel, layer=layer, G=G, H=H, LQ=LQ, PS=PS, NSEQ=NSEQ, scale=scale),
        grid_spec=grid_spec,
        out_shape=jax.ShapeDtypeStruct((B, LQ, H * D_V), F32),
        compiler_params=_cparams("arbitrary", "arbitrary"),
        name="attn_sample",
    )(page_table, q, cos2, sin2, cnew, rnew, wuk_t, wuv_h, cache_ckv, cache_krope_t)


def _mem_attn_kernel(q_ref, mk_ref, mv_ref, o_ref, *, NB, LQ, M, MH, HD, head_rows, scale):
    for n in range(NB):
        rows = slice(n * LQ, (n + 1) * LQ)
        for h in range(MH):
            cs = slice(h * HD, (h + 1) * HD)
            if head_rows:
                k = mk_ref[n, pl.ds(h, M, stride=MH), :]
                v = mv_ref[n, pl.ds(h, M, stride=MH), :]
            else:
                k, v = mk_ref[n, :, cs], mv_ref[n, :, cs]
            q = (q_ref[rows, cs] * scale).astype(BF16)
            s = _dot_nt(q, k.astype(BF16))
            m = jnp.max(s, axis=-1, keepdims=True)
            p = jnp.exp(s - m)
            denom = jnp.sum(p, axis=-1, keepdims=True)
            o_ref[rows, cs] = _dot(p.astype(BF16), v.astype(BF16)) / denom


def _mem_attn(q, mk, mv, row0, n_seq, LQ, NB, mem_index, T_out, head_rows, name):
    C = q.shape[1]
    HD = LANES
    MH = C // HD
    M = mk.shape[-2] // MH if head_rows else mk.shape[-2]
    rb = NB * LQ
    assert row0 % rb == 0 and n_seq % NB == 0
    qspec = pl.BlockSpec((rb, C), lambda i: (row0 // rb + i, 0))
    lead = mk.ndim - 3
    mspec = pl.BlockSpec((None,) * lead + (NB,) + mk.shape[-2:], mem_index)
    return pl.pallas_call(
        functools.partial(_mem_attn_kernel, NB=NB, LQ=LQ, M=M, MH=MH, HD=HD, head_rows=head_rows,
                          scale=float(HD ** -0.5)),
        grid=(n_seq // NB,),
        in_specs=[qspec, mspec, mspec],
        out_specs=pl.BlockSpec((rb, C), lambda i: (i, 0)),
        out_shape=jax.ShapeDtypeStruct((T_out, C), F32),
        compiler_params=_cparams("parallel"),
        name=name,
    )(q, mk, mv)


def _ffn_kernel(x_ref, g_ref, wg_ref, wu_ref, wd_ref, o_ref, xn_s):
    f = pl.program_id(1)

    @pl.when(f == 0)
    def _():
        x = x_ref[...]
        xn_s[...] = _rms(x, g_ref[...]).astype(BF16)
        o_ref[...] = x

    xn = xn_s[...]
    h = _silu(_dot(xn, wg_ref[...])) * _dot(xn, wu_ref[...])
    o_ref[...] += _dot(h.astype(BF16), wd_ref[...])


def _ffn(x, g, wg, wu, wd, tm, tf):
    T, D = x.shape
    F = wg.shape[1]
    return pl.pallas_call(
        _ffn_kernel,
        grid=(T // tm, F // tf),
        in_specs=[pl.BlockSpec((tm, D), lambda i, f: (i, 0)), _resident((1, D)),
                  pl.BlockSpec((D, tf), lambda i, f: (0, f)), pl.BlockSpec((D, tf), lambda i, f: (0, f)),
                  pl.BlockSpec((tf, D), lambda i, f: (f, 0))],
        out_specs=pl.BlockSpec((tm, D), lambda i, f: (i, 0)),
        out_shape=jax.ShapeDtypeStruct((T, D), F32),
        scratch_shapes=[pltpu.VMEM((tm, D), BF16)],
        compiler_params=_cparams("parallel", "arbitrary"),
        name="ffn_dense",
    )(x, g, wg, wu, wd)


def _slab_load(ref, s, n, S):
    return ref[pl.ds(s, n, stride=S), :]


def _slab_store(ref, s, n, S, val):
    ref[pl.ds(s, n, stride=S), :] = val


def _pack_bf16_pair(a, b):
    lo = lax.bitcast_convert_type(a.astype(BF16).astype(F32), jnp.uint32)
    hi = lax.bitcast_convert_type(b.astype(BF16).astype(F32), jnp.uint32)
    return lax.shift_right_logical(lo, jnp.uint32(16)) | (hi & jnp.uint32(0xFFFF0000))


def _unpack_bf16_pair(w):
    a = lax.bitcast_convert_type(lax.shift_left(w, jnp.uint32(16)), F32)
    b = lax.bitcast_convert_type(w & jnp.uint32(0xFFFF0000), F32)
    return a.astype(BF16), b.astype(BF16)


def _router_kernel(x_ref, g_ref, wr_ref, xn_ref, info_ref, *, E):
    xn = _rms(x_ref[...], g_ref[...])
    tm, D = xn.shape
    S = D // (2 * LANES)
    for s in range(S):
        lo = xn[:, s * LANES:(s + 1) * LANES]
        hi = xn[:, D // 2 + s * LANES:D // 2 + (s + 1) * LANES]
        _slab_store(xn_ref, s, tm, S, _pack_bf16_pair(lo, hi))
    logits = jnp.dot(xn, wr_ref[...], preferred_element_type=F32, precision=lax.Precision.HIGHEST)
    lane = lax.broadcasted_iota(jnp.int32, logits.shape, 1).astype(F32)
    valid = lane < E
    logits = jnp.where(valid, logits, NEG)
    ex = jnp.exp(logits - jnp.max(logits, axis=-1, keepdims=True))
    probs = jnp.where(valid, ex / jnp.sum(ex, axis=-1, keepdims=True), -1.0)
    big = 1e9
    p1 = jnp.max(probs, axis=-1, keepdims=True)
    i1 = jnp.min(jnp.where(probs == p1, lane, big), axis=-1, keepdims=True)
    rest = jnp.where(lane == i1, -1.0, probs)
    p2 = jnp.max(rest, axis=-1, keepdims=True)
    i2 = jnp.min(jnp.where(rest == p2, lane, big), axis=-1, keepdims=True)
    tot = p1 + p2
    w1, w2 = p1 / tot, p2 / tot
    info = jnp.where(lane == i1, w1, 0.0) + jnp.where(lane == i2, w2, 0.0)
    for k, v in enumerate((i1, i2, w1, w2)):
        info = jnp.where(lane == E + k, v, info)
    info_ref[...] = info


def _router(x, g, wr_p, E, tm):
    T, D = x.shape
    return pl.pallas_call(
        functools.partial(_router_kernel, E=E),
        grid=(T // tm,),
        in_specs=[pl.BlockSpec((tm, D), lambda i: (i, 0)), _resident((1, D)), _resident(wr_p.shape)],
        out_specs=[pl.BlockSpec((tm * (D // (2 * LANES)), LANES), lambda i: (i, 0)),
                   pl.BlockSpec((tm, LANES), lambda i: (i, 0))],
        out_shape=[jax.ShapeDtypeStruct((T * (D // (2 * LANES)), LANES), jnp.uint32),
                   jax.ShapeDtypeStruct((T, LANES), F32)],
        compiler_params=_cparams("parallel"),
        name="moe_router",
    )(x, g, wr_p)


def _moe_dispatch_kernel(zrow_ref, xn_ref, dest_hbm, xs_hbm, dsm, zbuf, sem_d, sem_z, sem_r, *, E, tm, tme, S):
    i = pl.program_id(0)
    dcopy = pltpu.make_async_copy(dest_hbm.at[pl.ds(pl.multiple_of(i * 2 * tm, 2 * tm), 2 * tm)], dsm, sem_d)
    dcopy.start()

    @pl.when(i == 0)
    def _():
        zbuf[...] = jnp.zeros(zbuf.shape, zbuf.dtype)
        n_tiles = xs_hbm.shape[0] // (tme * S)
        for wait in (False, True):
            for e in range(E):
                for row, ok in ((zrow_ref[e], zrow_ref[e] >= 0),
                                ((zrow_ref[E] + e) * tme, zrow_ref[E] + e < n_tiles)):
                    @pl.when(ok)
                    def _():
                        r0 = pl.multiple_of(row * S, tme * S)
                        cp = pltpu.make_async_copy(zbuf, xs_hbm.at[pl.ds(r0, tme * S)], sem_z)
                        cp.wait() if wait else cp.start()

    dcopy.wait()

    def row_copy(t, k):
        src = xn_ref.at[pl.ds(pl.multiple_of(t * S, S), S)]
        dst = xs_hbm.at[pl.ds(pl.multiple_of(dsm[2 * t + k] * S, S), S)]
        return pltpu.make_async_copy(src, dst, sem_r)

    def issue(t, carry):
        row_copy(t, 0).start()
        row_copy(t, 1).start()
        return carry

    def drain(t, carry):
        row_copy(t, 0).wait()
        row_copy(t, 1).wait()
        return carry

    lax.fori_loop(0, tm, issue, 0, unroll=8)
    lax.fori_loop(0, tm, drain, 0, unroll=8)


def _moe_dispatch(xn, dest, zrow, P, D, tm, tme):
    S = D // (2 * LANES)
    T = xn.shape[0] // S
    E = zrow.shape[0] - 1
    grid_spec = pltpu.PrefetchScalarGridSpec(
        num_scalar_prefetch=1,
        grid=(T // tm,),
        in_specs=[pl.BlockSpec((tm * S, LANES), lambda i, z: (i, 0)), pl.BlockSpec(memory_space=pl.ANY)],
        out_specs=pl.BlockSpec(memory_space=pl.ANY),
        scratch_shapes=[pltpu.SMEM((2 * tm,), jnp.int32), pltpu.VMEM((tme * S, LANES), xn.dtype),
                        pltpu.SemaphoreType.DMA, pltpu.SemaphoreType.DMA, pltpu.SemaphoreType.DMA],
    )
    return pl.pallas_call(
        functools.partial(_moe_dispatch_kernel, E=E, tm=tm, tme=tme, S=S),
        grid_spec=grid_spec,
        out_shape=jax.ShapeDtypeStruct((P * S, LANES), xn.dtype),
        compiler_params=_cparams("arbitrary"),
        name="moe_dispatch",
    )(zrow, xn, dest)


def _moe_ffn_kernel(te_ref, nu_ref, x_ref, wg_ref, wu_ref, wd_ref, o_ref, xn_s, acc_s, *, S):
    i, f = pl.program_id(0), pl.program_id(1)
    used = i < nu_ref[0]
    tme, D = xn_s.shape

    @pl.when(used)
    def _():
        @pl.when(f == 0)
        def _():
            for s in range(S // 2):
                lo, hi = _unpack_bf16_pair(_slab_load(x_ref, s, tme, S // 2))
                xn_s[:, s * LANES:(s + 1) * LANES] = lo
                xn_s[:, D // 2 + s * LANES:D // 2 + (s + 1) * LANES] = hi
            acc_s[...] = jnp.zeros(acc_s.shape, F32)

        xn = xn_s[...]
        h = _silu(_dot(xn, wg_ref[...])) * _dot(xn, wu_ref[...])
        acc_s[...] += _dot(h.astype(BF16), wd_ref[...])

        @pl.when(f == pl.num_programs(1) - 1)
        def _():
            for s in range(S):
                _slab_store(o_ref, s, tme, S, acc_s[:, s * LANES:(s + 1) * LANES])

    @pl.when(jnp.logical_not(used) & (f == 0))
    def _():
        o_ref[...] = jnp.zeros(o_ref.shape, F32)


def _moe_ffn(xs, tile_expert, n_used, wg, wu, wd, D, tme, tf):
    S = D // LANES
    P = xs.shape[0] // (S // 2)
    nf = wg.shape[2] // tf
    feff = lambda i, f, nu: jnp.where(i < nu[0], f, nf - 1)
    grid_spec = pltpu.PrefetchScalarGridSpec(
        num_scalar_prefetch=2,
        grid=(P // tme, nf),
        in_specs=[pl.BlockSpec((tme * S // 2, LANES), lambda i, f, te, nu: (jnp.minimum(i, nu[0] - 1), 0)),
                  pl.BlockSpec((None, D, tf), lambda i, f, te, nu: (te[i], 0, feff(i, f, nu))),
                  pl.BlockSpec((None, D, tf), lambda i, f, te, nu: (te[i], 0, feff(i, f, nu))),
                  pl.BlockSpec((None, tf, D), lambda i, f, te, nu: (te[i], feff(i, f, nu), 0))],
        out_specs=pl.BlockSpec((tme * S, LANES), lambda i, f, te, nu: (i, 0)),
        scratch_shapes=[pltpu.VMEM((tme, D), BF16), pltpu.VMEM((tme, D), F32)],
    )
    return pl.pallas_call(
        functools.partial(_moe_ffn_kernel, S=S),
        grid_spec=grid_spec,
        out_shape=jax.ShapeDtypeStruct((P * S, LANES), F32),
        compiler_params=_cparams("arbitrary", "arbitrary"),
        name="moe_ffn",
    )(tile_expert, n_used, xs, wg, wu, wd)


def _moe_combine_kernel(x_ref, info_ref, dest_hbm, ys_hbm, o_ref, dsm, buf0, buf1, sem_d, sem_r, *, E, tm, S):
    i, n = pl.program_id(0), pl.num_programs(0)
    slot = i % 2
    bufs = (buf0, buf1)

    def row_copy(sl, t, k):
        src = ys_hbm.at[pl.ds(pl.multiple_of(dsm[sl * 2 * tm + 2 * t + k] * S, S), S)]
        return pltpu.make_async_copy(src, bufs[k].at[sl, pl.ds(pl.multiple_of(t * S, S), S)], sem_r.at[sl])

    def fetch(tile, sl):
        dcopy = pltpu.make_async_copy(dest_hbm.at[pl.ds(pl.multiple_of(tile * 2 * tm, 2 * tm), 2 * tm)],
                                      dsm.at[pl.ds(pl.multiple_of(sl * 2 * tm, 2 * tm), 2 * tm)], sem_d)
        dcopy.start()
        dcopy.wait()

        def issue(t, carry):
            row_copy(sl, t, 0).start()
            row_copy(sl, t, 1).start()
            return carry

        lax.fori_loop(0, tm, issue, 0, unroll=8)

    @pl.when(i == 0)
    def _():
        fetch(i, slot)

    @pl.when(i + 1 < n)
    def _():
        fetch(i + 1, 1 - slot)

    def drain(t, carry):
        row_copy(slot, t, 0).wait()
        row_copy(slot, t, 1).wait()
        return carry

    lax.fori_loop(0, tm, drain, 0, unroll=8)
    w1 = info_ref[:, E + 2:E + 3]
    w2 = info_ref[:, E + 3:E + 4]
    y0, y1 = buf0.at[slot], buf1.at[slot]
    for s in range(S):
        cs = slice(s * LANES, (s + 1) * LANES)
        o_ref[:, cs] = x_ref[:, cs] + w1 * _slab_load(y0, s, tm, S) + w2 * _slab_load(y1, s, tm, S)


def _moe_combine(x, info, dest, ys, E, tm):
    T, D = x.shape
    S = D // LANES
    return pl.pallas_call(
        functools.partial(_moe_combine_kernel, E=E, tm=tm, S=S),
        grid=(T // tm,),
        in_specs=[pl.BlockSpec((tm, D), lambda i: (i, 0)), pl.BlockSpec((tm, LANES), lambda i: (i, 0)),
                  pl.BlockSpec(memory_space=pl.ANY), pl.BlockSpec(memory_space=pl.ANY)],
        out_specs=pl.BlockSpec((tm, D), lambda i: (i, 0)),
        out_shape=jax.ShapeDtypeStruct((T, D), F32),
        scratch_shapes=[pltpu.SMEM((4 * tm,), jnp.int32), pltpu.VMEM((2, tm * S, LANES), F32),
                        pltpu.VMEM((2, tm * S, LANES), F32), pltpu.SemaphoreType.DMA,
                        pltpu.SemaphoreType.DMA((2,))],
        compiler_params=_cparams("arbitrary"),
        name="moe_combine",
    )(x, info, dest, ys)


def _moe_plan(info, E, tme, n_tiles):
    e_flat = info[:, E:E + 2].astype(jnp.int32).reshape(-1)
    onehot = (e_flat[:, None] == jnp.arange(E, dtype=jnp.int32)[None, :]).astype(jnp.int32)
    csum = jnp.cumsum(onehot, axis=0)
    counts = csum[-1]
    rank = jnp.take_along_axis(csum, e_flat[:, None], axis=1)[:, 0] - 1
    padded = ((counts + tme - 1) // tme) * tme
    ends = jnp.cumsum(padded)
    dest = (ends - padded)[e_flat] + rank
    n_used = (ends[-1] // tme).astype(jnp.int32)
    tile = jnp.arange(n_tiles, dtype=jnp.int32)
    tile_expert = jnp.sum((tile[:, None] * tme >= ends[None, :]).astype(jnp.int32), axis=1)
    tile_expert = jnp.minimum(tile_expert, E - 1)
    last_used = tile_expert[jnp.maximum(n_used - 1, 0)]
    tile_expert = jnp.where(tile < n_used, tile_expert, last_used)
    zrow = jnp.concatenate([jnp.where(counts > 0, ends - tme, -1), n_used.reshape(1)]).astype(jnp.int32)
    return dest.astype(jnp.int32), tile_expert.astype(jnp.int32), n_used.reshape(1), zrow


def _norm_kernel(x_ref, g_ref, o_ref):
    o_ref[...] = _rms(x_ref[...], g_ref[...])


def _final_norm(x, g, row0, n_rows, tm):
    D = x.shape[1]
    return pl.pallas_call(
        _norm_kernel,
        grid=(n_rows // tm,),
        in_specs=[pl.BlockSpec((tm, D), lambda i: (row0 // tm + i, 0)), _resident((1, D))],
        out_specs=pl.BlockSpec((tm, D), lambda i: (i, 0)),
        out_shape=jax.ShapeDtypeStruct((n_rows, D), F32),
        compiler_params=_cparams("parallel"),
        name="final_norm",
    )(x, g)


def _block_diag(w):
    n, c, d = w.shape
    eye = jnp.eye(n, dtype=w.dtype)
    return (w[:, :, None, :] * eye[:, None, :, None]).reshape(n * c, n * d)


def _pad_to(x, axis, mult):
    pad = (-x.shape[axis]) % mult
    if pad == 0:
        return x
    widths = [(0, 0)] * x.ndim
    widths[axis] = (0, pad)
    return jnp.pad(x, widths)


def kernel(x_prompt, x_sample, mem_prompt, cache_ckv, cache_krope, page_table, state_pool, state_conv, state_h, cache_mem_k, cache_mem_v, norm_mix, norm_mem, norm_memkv, norm_ffn, norm_final, w_in, w_pool, pool_scale, conv_w, conv_b, w_rg_a, b_rg_a, w_rg_x, b_rg_x, lru_lambda, q_norm, w_uq, kv_norm, w_uk, w_uv, w_out, w_mq, w_mk, w_mv, w_mo, ffn_w_gate, ffn_w_up, ffn_w_down, moe_router, moe_w_gate, moe_w_up, moe_w_down):
    B, L, D = x_prompt.shape
    BS, LS, _ = x_sample.shape
    depth = w_in.shape[0]
    d_pool = state_pool.shape[-1]
    d_rnn = state_conv.shape[-1]
    r_q = q_norm.shape[-1]
    r_kv = kv_norm.shape[-1]
    H = w_uq.shape[2]
    M, MH, MHD = cache_mem_k.shape[2:]
    E = moe_router.shape[-1]
    PS = cache_ckv.shape[2]
    n_pages = page_table.shape[1]
    past_len = n_pages * PS
    NP, NS = B * L, BS * LS
    T = NP + NS
    assert d_pool == len(POOL_WINDOWS) * LANES and MHD == LANES and w_uq.shape[3] == D_NOPE + D_ROPE

    tm = min(512, NS)
    assert NP % tm == 0 and NS % tm == 0
    tq = min(512, L)
    TL = min(512, L)
    G = max(1, min(16, n_pages // 2))
    assert n_pages % G == 0
    offs = (0, d_pool, d_pool + d_rnn, d_pool + 2 * d_rnn, d_pool + 2 * d_rnn + r_q,
            d_pool + 2 * d_rnn + r_q + r_kv)
    n_in = offs[-1] + D_ROPE
    offs = offs + (offs[-1] + LANES,)

    half = D_ROPE // 2
    freqs = ROPE_BASE ** (-jnp.arange(half, dtype=F32) / half)
    pos_p = jnp.arange(L, dtype=jnp.int32)
    pos_s = past_len + jnp.arange(LS, dtype=jnp.int32)

    def tables(pos):
        ang = pos.astype(F32)[:, None] * freqs[None, :]
        c, s = jnp.cos(ang), jnp.sin(ang)
        return jnp.concatenate([c, c], -1), jnp.concatenate([-s, s], -1)

    cos_p, sin_p = tables(pos_p)
    cos_s, sin_s = tables(pos_s)
    cos_all = jnp.concatenate([jnp.tile(cos_p, (B, 1)), jnp.tile(cos_s, (BS, 1))], 0)
    sin_all = jnp.concatenate([jnp.tile(sin_p, (B, 1)), jnp.tile(sin_s, (BS, 1))], 0)

    x = jnp.concatenate([x_prompt.reshape(NP, D), x_sample.reshape(NS, D)], 0)
    mem2 = mem_prompt.reshape(B * M, D)
    krope_t = jnp.swapaxes(cache_krope, 2, 3)
    cmk = cache_mem_k.reshape(depth, BS, M * MH, MHD)
    cmv = cache_mem_v.reshape(depth, BS, M * MH, MHD)
    row = lambda v: v.reshape(1, -1)

    outs = {k: [] for k in ('pc', 'pk', 'pp', 'pcv', 'ph', 'pmk', 'pmv', 'sc', 'sk', 'sp', 'scv', 'sh')}
    for l in range(depth):
        w_in_p = _pad_to(w_in[l], 1, LANES)
        w_in_p = jnp.pad(w_in_p, ((0, 0), (0, offs[-1] - w_in_p.shape[1]))).astype(BF16)
        wuq = w_uq[l]
        w_uq_p = jnp.concatenate([wuq[:, :, :D_NOPE].reshape(r_q, H * D_NOPE),
                                  wuq[:, :, D_NOPE:].reshape(r_q, H * D_ROPE)], 1).astype(BF16)
        wuk_t = jnp.transpose(w_uk[l], (1, 2, 0)).astype(BF16)
        wuv_h = jnp.transpose(w_uv[l], (1, 0, 2)).astype(BF16)
        lw = {
            'w_pool': w_pool[l].astype(BF16), 'pool_scale': row(pool_scale[l]),
            'conv_w': conv_w[l], 'conv_b': row(conv_b[l]),
            'w_rg_a_bd': _block_diag(w_rg_a[l]).astype(BF16), 'b_rg_a': row(b_rg_a[l]),
            'w_rg_x_bd': _block_diag(w_rg_x[l]).astype(BF16), 'b_rg_x': row(b_rg_x[l]),
            'lru_lambda': row(lru_lambda[l]),
        }

        u, xb, gb, q, ckv, kr = _in_proj(x, row(norm_mix[l]), w_in_p, row(q_norm[l]), w_uq_p,
                                         row(kv_norm[l]), cos_all, sin_all, offs, tm)

        y_pool, y_rnn, pool_new, conv_new, h_new = _seqmix_prompt(u, xb, gb, lw, B, L, TL)
        tmaj = lambda a: jnp.transpose(a[NP:].reshape(BS, LS, -1), (1, 0, 2))
        u_s, xb_s, gb_s = tmaj(u), tmaj(xb), tmaj(gb)
        yp_s, yr_s, h_s = _seqmix_sample(u_s, xb_s, gb_s, jnp.transpose(state_pool[l], (1, 0, 2)),
                                         jnp.transpose(state_conv[l], (1, 0, 2)), state_h[l], lw, past_len)
        bmaj = lambda a: jnp.transpose(a, (1, 0, 2)).reshape(NS, -1)

        y_mla = _attn_prompt(q, cos_all, sin_all, ckv, kr, wuk_t, wuv_h, B, L, tq)
        s_ckv = ckv[NP:].reshape(BS, LS, r_kv)
        s_kr = kr[NP:].reshape(BS, LS, D_ROPE)
        y_mla_s = _attn_sample(page_table, q[NP:].reshape(BS, LS, -1), cos_s, sin_s, s_ckv, s_kr,
                               wuk_t, wuv_h, cache_ckv, krope_t, l, G)
        x = _mm_res([(y_pool, bmaj(yp_s)), (y_rnn, bmaj(yr_s)), (y_mla, y_mla_s.reshape(NS, -1))],
                    w_out[l].astype(BF16), x, tm, "out_proj")

        w_mkv = jnp.concatenate([w_mk[l].reshape(D, MH * MHD), w_mv[l].reshape(D, MH * MHD)], 1).astype(BF16)
        mk_p, mv_p = _norm_mm(mem2, row(norm_memkv[l]), w_mkv,
                              ((0, MH * MHD), (MH * MHD, 2 * MH * MHD)), min(512, B * M), "mem_kv")
        (qm,) = _norm_mm(x, row(norm_mem[l]), w_mq[l].reshape(D, MH * MHD).astype(BF16),
                         ((0, MH * MHD),), tm, "mem_q")
        lq_p = min(512, L)
        o_mem = _mem_attn(qm, mk_p.reshape(B, M, -1), mv_p.reshape(B, M, -1), 0, NP // lq_p, lq_p, 1,
                          lambda i: (i // (L // lq_p), 0, 0), NP, False, "mem_attn_prompt")
        nb_s = 8 if BS % 8 == 0 else 1
        o_mem_s = _mem_attn(qm, cmk, cmv, NP, BS, LS, nb_s, lambda i: (l, i, 0, 0), NS, True,
                            "mem_attn_sample")
        x = _mm_res([(o_mem, o_mem_s)], w_mo[l].reshape(MH * MHD, D).astype(BF16), x, tm, "mem_out")

        j = l // 2
        if l % 2 == 0:
            x = _ffn(x, row(norm_ffn[l]), ffn_w_gate[j].astype(BF16), ffn_w_up[j].astype(BF16),
                     ffn_w_down[j].astype(BF16), tm, 512)
        else:
            wr_p = _pad_to(moe_router[j], 1, LANES)
            xn, info = _router(x, row(norm_ffn[l]), wr_p, E, tm)
            tf = 512
            wg = _pad_to(moe_w_gate[j], 2, tf).astype(BF16)
            wu = _pad_to(moe_w_up[j], 2, tf).astype(BF16)
            wd = _pad_to(moe_w_down[j], 1, tf).astype(BF16)
            tme = 512 if T >= 4096 else 128
            n_tiles = (2 * T + E * (tme - 1) + tme - 1) // tme
            dest, tile_expert, n_used, zrow = _moe_plan(info, E, tme, n_tiles)
            xs = _moe_dispatch(xn, dest, zrow, n_tiles * tme, D, tm, tme)
            ys = _moe_ffn(xs, tile_expert, n_used, wg, wu, wd, D, tme, tf)
            x = _moe_combine(x, info, dest, ys, E, tm)

        outs['pc'].append(ckv[:NP].reshape(B, L, r_kv))
        outs['pk'].append(kr[:NP].reshape(B, L, D_ROPE))
        outs['pp'].append(pool_new)
        outs['pcv'].append(conv_new)
        outs['ph'].append(h_new.reshape(B, d_rnn))
        outs['pmk'].append(mk_p.reshape(B, M, MH, MHD))
        outs['pmv'].append(mv_p.reshape(B, M, MH, MHD))
        outs['sc'].append(s_ckv)
        outs['sk'].append(s_kr)
        u_sb = u[NP:].reshape(BS, LS, d_pool)
        xb_sb = xb[NP:].reshape(BS, LS, d_rnn)
        outs['sp'].append(jnp.concatenate([state_pool[l], u_sb], 1)[:, -POOL_STATE:])
        outs['scv'].append(jnp.concatenate([state_conv[l], xb_sb], 1)[:, -(CONV_W - 1):])
        outs['sh'].append(h_s)

    y_prompt = _final_norm(x, row(norm_final), 0, NP, tm).reshape(B, L, D)
    y_sample = _final_norm(x, row(norm_final), NP, NS, tm).reshape(BS, LS, D)
    st = lambda k: jnp.stack(outs[k])
    return (y_prompt, y_sample, st('pc'), st('pk'), st('pp'), st('pcv'), st('ph'), st('pmk'), st('pmv'),
            st('sc'), st('sk'), st('sp'), st('scv'), st('sh'))
```

```python
import functools

import jax
import jax.numpy as jnp
from jax import lax
from jax.experimental import pallas as pl
from jax.experimental.pallas import tpu as pltpu

F32 = jnp.float32
BF16 = jnp.bfloat16

EPS = 1e-6
POOL_WINDOWS = (2, 4, 8, 16)
POOL_STATE = max(POOL_WINDOWS) - 1
CONV_W = 4
LRU_C = 8.0
ROPE_BASE = 10000.0
D_NOPE = 128
D_ROPE = 64
D_V = 128
LANES = 128
VMEM_LIMIT = 56 * 1024 * 1024
NEG = -1e30


def _cparams(*sem):
    return pltpu.CompilerParams(dimension_semantics=sem, vmem_limit_bytes=VMEM_LIMIT)


def _resident(shape):
    nd = len(shape)
    return pl.BlockSpec(shape, lambda *_: (0,) * nd, pipeline_mode=pl.Buffered(1))


def _rms(x, g):
    ms = jnp.mean(x * x, axis=-1, keepdims=True)
    return x * lax.rsqrt(ms + EPS) * g


def _dot(a, b):
    return jnp.dot(a, b, preferred_element_type=F32)


def _dot_nt(a, b):
    return lax.dot_general(a, b, (((1,), (1,)), ((), ())), preferred_element_type=F32)


def _rope64(x, cos2, sin2):
    half = x.shape[-1] // 2
    rot = jnp.concatenate([x[:, half:], x[:, :half]], axis=-1)
    return x * cos2 + rot * sin2


def _silu(x):
    return x * jax.nn.sigmoid(x)


def _gelu_tanh(x):
    return 0.5 * x * (1.0 + jnp.tanh(0.7978845608028654 * (x + 0.044715 * (x * x * x))))


def _softplus(x):
    return jnp.maximum(x, 0.0) + jnp.log1p(jnp.exp(-jnp.abs(x)))


def _in_proj_kernel(x_ref, g_ref, w_ref, qn_ref, wuq_ref, kvn_ref, cos_ref, sin_ref,
                    u_ref, xb_ref, gb_ref, q_ref, ckv_ref, kr_ref, *, offs):
    o_pool, o_rx, o_rg, o_cq, o_ckv, o_kr, o_end = offs
    xn = _rms(x_ref[...], g_ref[...]).astype(BF16)

    def mm(a, b):
        return _dot(xn, w_ref[:, a:b])

    u_ref[...] = mm(o_pool, o_rx)
    xb_ref[...] = mm(o_rx, o_rg)
    gb_ref[...] = mm(o_rg, o_cq)
    cq = mm(o_cq, o_ckv)
    q_ref[...] = _dot(_rms(cq, qn_ref[...]).astype(BF16), wuq_ref[...])
    kv = mm(o_ckv, o_end)
    r_kv = o_kr - o_ckv
    ckv_ref[...] = _rms(kv[:, :r_kv], kvn_ref[...])
    kr_ref[...] = _rope64(kv[:, r_kv:r_kv + D_ROPE], cos_ref[...], sin_ref[...])


def _in_proj(x, g, w_in_p, q_norm, w_uq_p, kv_norm, cos2, sin2, offs, tm):
    T, D = x.shape
    n_in = w_in_p.shape[1]
    d_pool = offs[1] - offs[0]
    d_rnn = offs[2] - offs[1]
    r_q = offs[4] - offs[3]
    r_kv = offs[5] - offs[4]
    nq = w_uq_p.shape[1]
    row = lambda w: pl.BlockSpec((tm, w), lambda i: (i, 0))
    return pl.pallas_call(
        functools.partial(_in_proj_kernel, offs=offs),
        grid=(T // tm,),
        in_specs=[row(D), _resident((1, D)), _resident((D, n_in)), _resident((1, r_q)),
                  _resident((r_q, nq)), _resident((1, r_kv)), row(D_ROPE), row(D_ROPE)],
        out_specs=[row(d_pool), row(d_rnn), row(d_rnn), row(nq), row(r_kv), row(D_ROPE)],
        out_shape=[jax.ShapeDtypeStruct((T, w), F32) for w in (d_pool, d_rnn, d_rnn, nq, r_kv, D_ROPE)],
        compiler_params=_cparams("parallel"),
        name="in_proj",
    )(x, g, w_in_p, q_norm, w_uq_p, kv_norm, cos2, sin2)


def _norm_mm_kernel(x_ref, g_ref, w_ref, *o_refs, splits):
    xn = _rms(x_ref[...], g_ref[...]).astype(BF16)
    for o_ref, (a, b) in zip(o_refs, splits):
        o_ref[...] = _dot(xn, w_ref[:, a:b])


def _norm_mm(x, g, w, splits, tm, name):
    T, K = x.shape
    return pl.pallas_call(
        functools.partial(_norm_mm_kernel, splits=splits),
        grid=(T // tm,),
        in_specs=[pl.BlockSpec((tm, K), lambda i: (i, 0)), _resident((1, K)), _resident(w.shape)],
        out_specs=[pl.BlockSpec((tm, b - a), lambda i: (i, 0)) for a, b in splits],
        out_shape=[jax.ShapeDtypeStruct((T, b - a), F32) for a, b in splits],
        compiler_params=_cparams("parallel"),
        name=name,
    )(x, g, w)


def _mm_res_kernel(*refs, n, n_ptiles):
    ap_refs, as_refs, w_refs = refs[:n], refs[n:2 * n], refs[2 * n:3 * n]
    res_ref, o_ref = refs[3 * n], refs[3 * n + 1]
    i = pl.program_id(0)

    def body(a_refs):
        acc = res_ref[...]
        for a_ref, w_ref in zip(a_refs, w_refs):
            acc = acc + _dot(a_ref[...].astype(BF16), w_ref[...])
        o_ref[...] = acc

    @pl.when(i < n_ptiles)
    def _():
        body(ap_refs)

    @pl.when(i >= n_ptiles)
    def _():
        body(as_refs)


def _mm_res(a_pairs, w, res, tm, name):
    T, N = res.shape
    n = len(a_pairs)
    n_ptiles = a_pairs[0][0].shape[0] // tm
    n_stiles = a_pairs[0][1].shape[0] // tm
    assert n_ptiles + n_stiles == T // tm
    p_specs, s_specs, w_specs, r0 = [], [], [], 0
    for a_p, a_s in a_pairs:
        k = a_p.shape[1]
        assert r0 % k == 0 and a_p.shape[0] == n_ptiles * tm and a_s.shape == (n_stiles * tm, k)
        p_specs.append(pl.BlockSpec((tm, k), lambda i: (jnp.minimum(i, n_ptiles - 1), 0)))
        s_specs.append(pl.BlockSpec((tm, k), lambda i: (jnp.maximum(i - n_ptiles, 0), 0)))
        w_specs.append(pl.BlockSpec((k, N), functools.partial(lambda blk, i: (blk, 0), r0 // k),
                                    pipeline_mode=pl.Buffered(1)))
        r0 += k
    return pl.pallas_call(
        functools.partial(_mm_res_kernel, n=n, n_ptiles=n_ptiles),
        grid=(T // tm,),
        in_specs=p_specs + s_specs + w_specs + [pl.BlockSpec((tm, N), lambda i: (i, 0))],
        out_specs=pl.BlockSpec((tm, N), lambda i: (i, 0)),
        out_shape=jax.ShapeDtypeStruct((T, N), F32),
        compiler_params=_cparams("parallel"),
        name=name,
    )(*[a for a, _ in a_pairs], *[a for _, a in a_pairs], *([w] * n), res)


def _lru_coeffs(xc, wa_ref, ba_ref, wx_ref, bx_ref, sp):
    xcb = xc.astype(BF16)
    r = jax.nn.sigmoid(_dot(xcb, wa_ref[...]) + ba_ref[...])
    i = jax.nn.sigmoid(_dot(xcb, wx_ref[...]) + bx_ref[...])
    log_a = -LRU_C * r * sp
    a = jnp.exp(log_a)
    b = jnp.sqrt(jnp.tanh(-log_a) * (1.0 + a * a)) * (i * xc)
    return a, b


def _seqmix_prompt_kernel(u_ref, xb_ref, gb_ref, wpool_ref, pscale_ref, convw_ref, convb_ref,
                          wa_ref, ba_ref, wx_ref, bx_ref, lam_ref,
                          ypool_ref, yrnn_ref, pool_new_ref, conv_new_ref, h_new_ref,
                          ext_ref, extc_ref, a_ref, b_ref, hc_ref, *, TL, CH):
    l = pl.program_id(1)
    PH, CHist = 16, 8

    @pl.when(l == 0)
    def _():
        ext_ref[0:PH, :] = jnp.zeros((PH, ext_ref.shape[1]), F32)
        extc_ref[0:CHist, :] = jnp.zeros((CHist, extc_ref.shape[1]), F32)
        hc_ref[...] = jnp.zeros(hc_ref.shape, F32)

    @pl.when(l > 0)
    def _():
        ext_ref[0:PH, :] = ext_ref[TL:TL + PH, :]
        extc_ref[0:CHist, :] = extc_ref[TL:TL + CHist, :]

    ext_ref[PH:PH + TL, :] = u_ref[...]
    extc_ref[CHist:CHist + TL, :] = xb_ref[...]

    sp = _softplus(-lam_ref[...])
    gw = LANES
    for c in range(TL // CH):
        r0 = c * CH
        t = l * TL + r0 + lax.broadcasted_iota(jnp.int32, (CH, 1), 0)
        for g, w in enumerate(POOL_WINDOWS):
            cs = slice(g * gw, (g + 1) * gw)
            acc = ext_ref[PH + r0:PH + r0 + CH, cs]
            for k in range(1, w):
                acc = acc + ext_ref[PH + r0 - k:PH + r0 - k + CH, cs]
            cnt = jnp.minimum(t + 1, w).astype(F32)
            p = acc / cnt - u_ref[r0:r0 + CH, cs]
            ypool_ref[r0:r0 + CH, cs] = _dot(p.astype(BF16), wpool_ref[g]) * pscale_ref[:, cs]
        xc = convb_ref[...] + convw_ref[0:1, :] * extc_ref[CHist + r0 - (CONV_W - 1):CHist + r0 - (CONV_W - 1) + CH, :]
        for k in range(1, CONV_W):
            o = CHist + r0 - (CONV_W - 1) + k
            xc = xc + convw_ref[k:k + 1, :] * extc_ref[o:o + CH, :]
        a, b = _lru_coeffs(xc, wa_ref, ba_ref, wx_ref, bx_ref, sp)
        a_ref[r0:r0 + CH, :] = a
        b_ref[r0:r0 + CH, :] = b

    def step(i, h):
        o = pl.multiple_of(i * 8, 8)
        a8 = a_ref[pl.ds(o, 8), :]
        b8 = b_ref[pl.ds(o, 8), :]
        rows = []
        for r in range(8):
            h = a8[r:r + 1, :] * h + b8[r:r + 1, :]
            rows.append(h)
        b_ref[pl.ds(o, 8), :] = jnp.concatenate(rows, axis=0)
        return h

    h_last = lax.fori_loop(0, TL // 8, step, hc_ref[0:1, :])
    hc_ref[0:1, :] = h_last
    yrnn_ref[...] = _gelu_tanh(gb_ref[...]) * b_ref[...]
    pool_new_ref[...] = ext_ref[PH + TL - POOL_STATE:PH + TL, :]
    conv_new_ref[...] = extc_ref[CHist + TL - (CONV_W - 1):CHist + TL, :]
    h_new_ref[...] = h_last


def _seqmix_prompt(u, xb, gb, lw, B, L, TL):
    T, C = u.shape
    nl = L // TL
    CH = min(128, TL)
    seq = lambda w: pl.BlockSpec((TL, w), lambda b, l: (b * nl + l, 0))
    state = lambda r: pl.BlockSpec((None, r, C), lambda b, l: (b, 0, 0))
    return pl.pallas_call(
        functools.partial(_seqmix_prompt_kernel, TL=TL, CH=CH),
        grid=(B, nl),
        in_specs=[seq(C), seq(C), seq(C), _resident(lw['w_pool'].shape), _resident((1, C)),
                  _resident((CONV_W, C)), _resident((1, C)), _resident((C, C)), _resident((1, C)),
                  _resident((C, C)), _resident((1, C)), _resident((1, C))],
        out_specs=[seq(C), seq(C), state(POOL_STATE), state(CONV_W - 1), state(1)],
        out_shape=[jax.ShapeDtypeStruct((B * L, C), F32), jax.ShapeDtypeStruct((B * L, C), F32),
                   jax.ShapeDtypeStruct((B, POOL_STATE, C), F32),
                   jax.ShapeDtypeStruct((B, CONV_W - 1, C), F32),
                   jax.ShapeDtypeStruct((B, 1, C), F32)],
        scratch_shapes=[pltpu.VMEM((16 + TL, C), F32), pltpu.VMEM((8 + TL, C), F32),
                        pltpu.VMEM((TL, C), F32), pltpu.VMEM((TL, C), F32), pltpu.VMEM((8, C), F32)],
        compiler_params=_cparams("parallel", "arbitrary"),
        name="seqmix_prompt",
    )(u, xb, gb, lw['w_pool'], lw['pool_scale'], lw['conv_w'], lw['conv_b'],
      lw['w_rg_a_bd'], lw['b_rg_a'], lw['w_rg_x_bd'], lw['b_rg_x'], lw['lru_lambda'])


def _seqmix_sample_kernel(u_ref, xb_ref, gb_ref, sp_ref, sc_ref, h0_ref, wpool_ref, pscale_ref,
                          convw_ref, convb_ref, wa_ref, ba_ref, wx_ref, bx_ref, lam_ref,
                          ypool_ref, yrnn_ref, h_new_ref, *, LQ, pos0):
    gw = LANES
    sp = _softplus(-lam_ref[...])

    def ext(j, cs):
        return sp_ref[j, :, cs] if j < POOL_STATE else u_ref[j - POOL_STATE, :, cs]

    def extc(j):
        return sc_ref[j] if j < CONV_W - 1 else xb_ref[j - (CONV_W - 1)]

    h = h0_ref[...]
    for t in range(LQ):
        for g, w in enumerate(POOL_WINDOWS):
            cs = slice(g * gw, (g + 1) * gw)
            e = POOL_STATE + t
            acc = ext(e, cs)
            for k in range(1, w):
                acc = acc + ext(e - k, cs)
            cnt = float(min(pos0 + t + 1, w))
            p = acc / cnt - u_ref[t, :, cs]
            ypool_ref[t, :, cs] = _dot(p.astype(BF16), wpool_ref[g]) * pscale_ref[:, cs]
        xc = convb_ref[...] + convw_ref[0:1, :] * extc(t)
        for k in range(1, CONV_W):
            xc = xc + convw_ref[k:k + 1, :] * extc(t + k)
        a, b = _lru_coeffs(xc, wa_ref, ba_ref, wx_ref, bx_ref, sp)
        h = a * h + b
        yrnn_ref[t] = _gelu_tanh(gb_ref[t]) * h
    h_new_ref[...] = h


def _seqmix_sample(u, xb, gb, st_pool, st_conv, h0, lw, pos0):
    LQ, B, C = u.shape
    BT = B // 2 if B % 16 == 0 else B
    tb = lambda r: pl.BlockSpec((r, BT, C), lambda i: (0, i, 0))
    return pl.pallas_call(
        functools.partial(_seqmix_sample_kernel, LQ=LQ, pos0=pos0),
        grid=(B // BT,),
        in_specs=[tb(LQ), tb(LQ), tb(LQ), tb(POOL_STATE), tb(CONV_W - 1),
                  pl.BlockSpec((BT, C), lambda i: (i, 0)),
                  _resident(lw['w_pool'].shape), _resident((1, C)), _resident((CONV_W, C)),
                  _resident((1, C)), _resident((C, C)), _resident((1, C)), _resident((C, C)),
                  _resident((1, C)), _resident((1, C))],
        out_specs=[tb(LQ), tb(LQ), pl.BlockSpec((BT, C), lambda i: (i, 0))],
        out_shape=[jax.ShapeDtypeStruct((LQ, B, C), F32), jax.ShapeDtypeStruct((LQ, B, C), F32),
                   jax.ShapeDtypeStruct((B, C), F32)],
        compiler_params=_cparams("parallel"),
        name="seqmix_sample",
    )(u, xb, gb, st_pool, st_conv, h0, lw['w_pool'], lw['pool_scale'], lw['conv_w'], lw['conv_b'],
      lw['w_rg_a_bd'], lw['b_rg_a'], lw['w_rg_x_bd'], lw['b_rg_x'], lw['lru_lambda'])


def _lane_fold(x, op):
    out = x[:, :LANES]
    for c in range(1, x.shape[1] // LANES):
        out = op(out, x[:, c * LANES:(c + 1) * LANES])
    return out


def _row_max(x):
    return jnp.max(_lane_fold(x, jnp.maximum), axis=-1, keepdims=True)


def _row_sum(x):
    return jnp.sum(_lane_fold(x, jnp.add), axis=-1, keepdims=True)


def _absorb_queries(q_ref, cos, sin, wuk_ref, qa_s, qr_s, H, rows, scale):
    n_nope = H * D_NOPE
    for h in range(H):
        qn = q_ref[:, h * D_NOPE:(h + 1) * D_NOPE].astype(BF16)
        qa = _dot(qn, wuk_ref[h]) * scale
        qr = _rope64(q_ref[:, n_nope + h * D_ROPE:n_nope + (h + 1) * D_ROPE], cos, sin) * scale
        if qa_s.ndim == 3:
            qa_s[h] = qa.astype(BF16)
            qr_s[h] = qr.astype(BF16)
        else:
            qa_s[h * rows:(h + 1) * rows, :] = qa.astype(BF16)
            qr_s[h * rows:(h + 1) * rows, :] = qr.astype(BF16)


def _attn_prompt_kernel(q_ref, cos_ref, sin_ref, ckv_ref, kr_ref, wuk_ref, wuv_ref, y_ref,
                        qa_s, qr_s, m_s, l_s, acc_s, *, H, scale):
    qi, ki = pl.program_id(1), pl.program_id(2)
    tq, tk = q_ref.shape[0], ckv_ref.shape[0]

    @pl.when(ki == 0)
    def _():
        _absorb_queries(q_ref, cos_ref[...], sin_ref[...], wuk_ref, qa_s, qr_s, H, tq, scale)
        m_s[...] = jnp.full(m_s.shape, NEG, F32)
        l_s[...] = jnp.zeros(l_s.shape, F32)
        acc_s[...] = jnp.zeros(acc_s.shape, F32)

    def update(masked):
        k = ckv_ref[...].astype(BF16)
        r = kr_ref[...].astype(BF16)
        if masked:
            keep = (lax.broadcasted_iota(jnp.int32, (tq, tk), 1)
                    <= lax.broadcasted_iota(jnp.int32, (tq, tk), 0))

        for h in range(H):
            s = _dot_nt(qa_s[h], k) + _dot_nt(qr_s[h], r)
            if masked:
                s = jnp.where(keep, s, NEG)
            m_prev = m_s[h]
            m_new = jnp.maximum(m_prev, jnp.broadcast_to(_row_max(s), m_prev.shape))
            alpha = jnp.exp(m_prev - m_new)
            ps = [jnp.exp(s[:, c * LANES:(c + 1) * LANES] - m_new) for c in range(tk // LANES)]
            psum = ps[0]
            for pc in ps[1:]:
                psum = psum + pc
            l_s[h] = alpha * l_s[h] + psum
            pv = _dot(jnp.concatenate(ps, axis=1).astype(BF16), k)
            acc = acc_s[h]
            acc_s[h] = jnp.concatenate(
                [alpha * acc[:, c * LANES:(c + 1) * LANES] for c in range(acc.shape[1] // LANES)], axis=1) + pv
            m_s[h] = m_new

    @pl.when(ki < qi)
    def _():
        update(False)

    @pl.when(ki == qi)
    def _():
        update(True)
        for h in range(H):
            o = acc_s[h] / jnp.sum(l_s[h], axis=-1, keepdims=True)
            y_ref[:, h * D_V:(h + 1) * D_V] = _dot(o.astype(BF16), wuv_ref[h])


def _attn_prompt(q, cos2, sin2, ckv, kr, wuk_t, wuv_h, B, L, tq):
    T = q.shape[0]
    H, _, r_kv = wuk_t.shape
    nq = L // tq
    scale = float((D_NOPE + D_ROPE) ** -0.5)
    qrow = lambda w: pl.BlockSpec((tq, w), lambda b, i, j: (b * nq + i, 0))
    krow = lambda w: pl.BlockSpec((tq, w), lambda b, i, j: (b * nq + jnp.minimum(i, j), 0))
    return pl.pallas_call(
        functools.partial(_attn_prompt_kernel, H=H, scale=scale),
        grid=(B, nq, nq),
        in_specs=[qrow(q.shape[1]), qrow(D_ROPE), qrow(D_ROPE), krow(r_kv), krow(D_ROPE),
                  _resident(wuk_t.shape), _resident(wuv_h.shape)],
        out_specs=qrow(H * D_V),
        out_shape=jax.ShapeDtypeStruct((B * L, H * D_V), F32),
        scratch_shapes=[pltpu.VMEM((H, tq, r_kv), BF16), pltpu.VMEM((H, tq, D_ROPE), BF16),
                        pltpu.VMEM((H, tq, LANES), F32), pltpu.VMEM((H, tq, LANES), F32),
                        pltpu.VMEM((H, tq, r_kv), F32)],
        compiler_params=_cparams("parallel", "parallel", "arbitrary"),
        name="attn_prompt",
    )(q, cos2, sin2, ckv, kr, wuk_t, wuv_h)


def _attn_sample_kernel(pt_ref, q_ref, cos_ref, sin_ref, cnew_ref, rnew_ref, wuk_ref, wuv_ref,
                        ckv_hbm, kr_hbm, y_ref,
                        kbuf, rbuf, ksem, rsem, qa_s, qr_s, m_s, l_s, acc_s,
                        *, layer, G, H, LQ, PS, NSEQ, scale):
    b, j = pl.program_id(0), pl.program_id(1)
    nb, nch = pl.num_programs(0), pl.num_programs(1)
    c = b * nch + j
    slot = c % 2

    def page_copies(bb, jj, sl):
        copies = []
        for n in range(NSEQ):
            for g in range(G):
                pg = pt_ref[bb * NSEQ + n, jj * G + g]
                copies.append(pltpu.make_async_copy(ckv_hbm.at[layer, pg], kbuf.at[sl, n, pl.ds(g * PS, PS)],
                                                    ksem.at[sl]))
                copies.append(pltpu.make_async_copy(kr_hbm.at[layer, pg], rbuf.at[sl, n, :, pl.ds(g * PS, PS)],
                                                    rsem.at[sl]))
        return copies

    @pl.when(c == 0)
    def _():
        for cp in page_copies(b, j, slot):
            cp.start()

    @pl.when(c + 1 < nb * nch)
    def _():
        last = j == nch - 1
        for cp in page_copies(jnp.where(last, b + 1, b), jnp.where(last, 0, j + 1), 1 - slot):
            cp.start()

    @pl.when(j == 0)
    def _():
        for n in range(NSEQ):
            _absorb_queries(q_ref.at[n], cos_ref[...], sin_ref[...], wuk_ref, qa_s.at[n], qr_s.at[n], H, LQ, scale)
        m_s[...] = jnp.full(m_s.shape, NEG, F32)
        l_s[...] = jnp.zeros(l_s.shape, F32)
        acc_s[...] = jnp.zeros(acc_s.shape, F32)

    for cp in page_copies(b, j, slot):
        cp.wait()

    ks = [kbuf[slot, n].astype(BF16) for n in range(NSEQ)]
    ss = [_dot_nt(qa_s[n], ks[n]) + _dot(qr_s[n], rbuf[slot, n].astype(BF16)) for n in range(NSEQ)]
    for n in range(NSEQ):
        m_prev = m_s[n]
        m_new = jnp.maximum(m_prev, _row_max(ss[n]))
        alpha = jnp.exp(m_prev - m_new)
        p = jnp.exp(ss[n] - m_new)
        l_s[n] = alpha * l_s[n] + _row_sum(p)
        acc_s[n] = alpha * acc_s[n] + _dot(p.astype(BF16), ks[n])
        m_s[n] = m_new

    @pl.when(j == nch - 1)
    def _():
        rows = H * LQ
        qpos = lax.broadcasted_iota(jnp.int32, (rows, 1), 0) % LQ
        for n in range(NSEQ):
            qa = qa_s[n].astype(F32)
            qr = qr_s[n].astype(F32)
            cols = []
            for t in range(LQ):
                st = (jnp.sum(qa * cnew_ref[n, t:t + 1, :], axis=-1, keepdims=True)
                      + jnp.sum(qr * rnew_ref[n, t:t + 1, :], axis=-1, keepdims=True))
                cols.append(jnp.where(qpos >= t, st, NEG))
            m_prev = m_s[n]
            m_new = m_prev
            for st in cols:
                m_new = jnp.maximum(m_new, st)
            alpha = jnp.exp(m_prev - m_new)
            lsum = alpha * l_s[n]
            acc = alpha * acc_s[n]
            for t, st in enumerate(cols):
                pt = jnp.exp(st - m_new)
                lsum = lsum + pt
                acc = acc + pt * cnew_ref[n, t:t + 1, :]
            o = (acc / lsum).astype(BF16)
            for h in range(H):
                y_ref[n, :, h * D_V:(h + 1) * D_V] = _dot(o[h * LQ:(h + 1) * LQ, :], wuv_ref[h])


def _attn_sample(page_table, q, cos2, sin2, cnew, rnew, wuk_t, wuv_h, cache_ckv, cache_krope_t, layer, G):
    B, LQ, nq = q.shape
    H, _, r_kv = wuk_t.shape
    PS = cache_ckv.shape[2]
    n_pages = page_table.shape[1]
    nch = n_pages // G
    NSEQ = 4 if B % 4 == 0 else 1
    R = H * LQ
    scale = float((D_NOPE + D_ROPE) ** -0.5)
    seq = lambda w: pl.BlockSpec((NSEQ, LQ, w), lambda b, j, pt: (b, 0, 0))
    const = lambda shape: pl.BlockSpec(shape, lambda b, j, pt: (0,) * len(shape), pipeline_mode=pl.Buffered(1))
    grid_spec = pltpu.PrefetchScalarGridSpec(
        num_scalar_prefetch=1,
        grid=(B // NSEQ, nch),
        in_specs=[seq(nq), const((LQ, D_ROPE)), const((LQ, D_ROPE)), seq(r_kv), seq(D_ROPE),
                  const(wuk_t.shape), const(wuv_h.shape),
                  pl.BlockSpec(memory_space=pl.ANY), pl.BlockSpec(memory_space=pl.ANY)],
        out_specs=seq(H * D_V),
        scratch_shapes=[pltpu.VMEM((2, NSEQ, G * PS, r_kv), F32), pltpu.VMEM((2, NSEQ, D_ROPE, G * PS), F32),
                        pltpu.SemaphoreType.DMA((2,)), pltpu.SemaphoreType.DMA((2,)),
                        pltpu.VMEM((NSEQ, R, r_kv), BF16), pltpu.VMEM((NSEQ, R, D_ROPE), BF16),
                        pltpu.VMEM((NSEQ, R, 1), F32), pltpu.VMEM((NSEQ, R, 1), F32),
                        pltpu.VMEM((NSEQ, R, r_kv), F32)],
    )
    return pl.pallas_call(
        functools.partial(_attn_sample_kernel, layer=layer, G=G, H=H, LQ=LQ, PS=PS, NSEQ=NSEQ, scale=scale),
        grid_spec=grid_spec,
        out_shape=jax.ShapeDtypeStruct((B, LQ, H * D_V), F32),
        compiler_params=_cparams("arbitrary", "arbitrary"),
        name="attn_sample",
    )(page_table, q, cos2, sin2, cnew, rnew, wuk_t, wuv_h, cache_ckv, cache_krope_t)


def _mem_attn_kernel(q_ref, mk_ref, mv_ref, o_ref, *, NB, LQ, M, MH, HD, head_rows, scale):
    for n in range(NB):
        rows = slice(n * LQ, (n + 1) * LQ)
        for h in range(MH):
            cs = slice(h * HD, (h + 1) * HD)
            if head_rows:
                k = mk_ref[n, pl.ds(h, M, stride=MH), :]
                v = mv_ref[n, pl.ds(h, M, stride=MH), :]
            else:
                k, v = mk_ref[n, :, cs], mv_ref[n, :, cs]
            q = (q_ref[rows, cs] * scale).astype(BF16)
            s = _dot_nt(q, k.astype(BF16))
            m = jnp.max(s, axis=-1, keepdims=True)
            p = jnp.exp(s - m)
            denom = jnp.sum(p, axis=-1, keepdims=True)
            o_ref[rows, cs] = _dot(p.astype(BF16), v.astype(BF16)) / denom


def _mem_attn(q, mk, mv, row0, n_seq, LQ, NB, mem_index, T_out, head_rows, name):
    C = q.shape[1]
    HD = LANES
    MH = C // HD
    M = mk.shape[-2] // MH if head_rows else mk.shape[-2]
    rb = NB * LQ
    assert row0 % rb == 0 and n_seq % NB == 0
    qspec = pl.BlockSpec((rb, C), lambda i: (row0 // rb + i, 0))
    lead = mk.ndim - 3
    mspec = pl.BlockSpec((None,) * lead + (NB,) + mk.shape[-2:], mem_index)
    return pl.pallas_call(
        functools.partial(_mem_attn_kernel, NB=NB, LQ=LQ, M=M, MH=MH, HD=HD, head_rows=head_rows,
                          scale=float(HD ** -0.5)),
        grid=(n_seq // NB,),
        in_specs=[qspec, mspec, mspec],
        out_specs=pl.BlockSpec((rb, C), lambda i: (i, 0)),
        out_shape=jax.ShapeDtypeStruct((T_out, C), F32),
        compiler_params=_cparams("parallel"),
        name=name,
    )(q, mk, mv)


def _ffn_kernel(x_ref, g_ref, wg_ref, wu_ref, wd_ref, o_ref, xn_s):
    f = pl.program_id(1)

    @pl.when(f == 0)
    def _():
        x = x_ref[...]
        xn_s[...] = _rms(x, g_ref[...]).astype(BF16)
        o_ref[...] = x

    xn = xn_s[...]
    h = _silu(_dot(xn, wg_ref[...])) * _dot(xn, wu_ref[...])
    o_ref[...] += _dot(h.astype(BF16), wd_ref[...])


def _ffn(x, g, wg, wu, wd, tm, tf):
    T, D = x.shape
    F = wg.shape[1]
    return pl.pallas_call(
        _ffn_kernel,
        grid=(T // tm, F // tf),
        in_specs=[pl.BlockSpec((tm, D), lambda i, f: (i, 0)), _resident((1, D)),
                  pl.BlockSpec((D, tf), lambda i, f: (0, f)), pl.BlockSpec((D, tf), lambda i, f: (0, f)),
                  pl.BlockSpec((tf, D), lambda i, f: (f, 0))],
        out_specs=pl.BlockSpec((tm, D), lambda i, f: (i, 0)),
        out_shape=jax.ShapeDtypeStruct((T, D), F32),
        scratch_shapes=[pltpu.VMEM((tm, D), BF16)],
        compiler_params=_cparams("parallel", "arbitrary"),
        name="ffn_dense",
    )(x, g, wg, wu, wd)


def _slab_load(ref, s, n, S):
    return ref[pl.ds(s, n, stride=S), :]


def _slab_store(ref, s, n, S, val):
    ref[pl.ds(s, n, stride=S), :] = val


def _router_kernel(x_ref, g_ref, wr_ref, xn_ref, info_ref, *, E):
    xn = _rms(x_ref[...], g_ref[...])
    tm, D = xn.shape
    S = D // LANES
    for s in range(S):
        _slab_store(xn_ref, s, tm, S, xn[:, s * LANES:(s + 1) * LANES])
    logits = jnp.dot(xn, wr_ref[...], preferred_element_type=F32, precision=lax.Precision.HIGHEST)
    lane = lax.broadcasted_iota(jnp.int32, logits.shape, 1).astype(F32)
    valid = lane < E
    logits = jnp.where(valid, logits, NEG)
    ex = jnp.exp(logits - jnp.max(logits, axis=-1, keepdims=True))
    probs = jnp.where(valid, ex / jnp.sum(ex, axis=-1, keepdims=True), -1.0)
    big = 1e9
    p1 = jnp.max(probs, axis=-1, keepdims=True)
    i1 = jnp.min(jnp.where(probs == p1, lane, big), axis=-1, keepdims=True)
    rest = jnp.where(lane == i1, -1.0, probs)
    p2 = jnp.max(rest, axis=-1, keepdims=True)
    i2 = jnp.min(jnp.where(rest == p2, lane, big), axis=-1, keepdims=True)
    tot = p1 + p2
    w1, w2 = p1 / tot, p2 / tot
    info = jnp.where(lane == i1, w1, 0.0) + jnp.where(lane == i2, w2, 0.0)
    for k, v in enumerate((i1, i2, w1, w2)):
        info = jnp.where(lane == E + k, v, info)
    info_ref[...] = info


def _router(x, g, wr_p, E, tm):
    T, D = x.shape
    return pl.pallas_call(
        functools.partial(_router_kernel, E=E),
        grid=(T // tm,),
        in_specs=[pl.BlockSpec((tm, D), lambda i: (i, 0)), _resident((1, D)), _resident(wr_p.shape)],
        out_specs=[pl.BlockSpec((tm * (D // LANES), LANES), lambda i: (i, 0)),
                   pl.BlockSpec((tm, LANES), lambda i: (i, 0))],
        out_shape=[jax.ShapeDtypeStruct((T * (D // LANES), LANES), F32),
                   jax.ShapeDtypeStruct((T, LANES), F32)],
        compiler_params=_cparams("parallel"),
        name="moe_router",
    )(x, g, wr_p)


def _moe_dispatch_kernel(zrow_ref, xn_ref, dest_hbm, xs_hbm, dsm, zbuf, sem_d, sem_z, sem_r, *, E, tm, tme, S):
    i = pl.program_id(0)
    dcopy = pltpu.make_async_copy(dest_hbm.at[pl.ds(pl.multiple_of(i * 2 * tm, 2 * tm), 2 * tm)], dsm, sem_d)
    dcopy.start()

    @pl.when(i == 0)
    def _():
        zbuf[...] = jnp.zeros(zbuf.shape, zbuf.dtype)
        n_tiles = xs_hbm.shape[0] // (tme * S)
        for wait in (False, True):
            for e in range(E):
                for row, ok in ((zrow_ref[e], zrow_ref[e] >= 0),
                                ((zrow_ref[E] + e) * tme, zrow_ref[E] + e < n_tiles)):
                    @pl.when(ok)
                    def _():
                        r0 = pl.multiple_of(row * S, tme * S)
                        cp = pltpu.make_async_copy(zbuf, xs_hbm.at[pl.ds(r0, tme * S)], sem_z)
                        cp.wait() if wait else cp.start()

    dcopy.wait()

    def row_copy(t, k):
        src = xn_ref.at[pl.ds(pl.multiple_of(t * S, S), S)]
        dst = xs_hbm.at[pl.ds(pl.multiple_of(dsm[2 * t + k] * S, S), S)]
        return pltpu.make_async_copy(src, dst, sem_r)

    def issue(t, carry):
        row_copy(t, 0).start()
        row_copy(t, 1).start()
        return carry

    def drain(t, carry):
        row_copy(t, 0).wait()
        row_copy(t, 1).wait()
        return carry

    lax.fori_loop(0, tm, issue, 0, unroll=8)
    lax.fori_loop(0, tm, drain, 0, unroll=8)


def _moe_dispatch(xn, dest, zrow, P, D, tm, tme):
    S = D // LANES
    T = xn.shape[0] // S
    E = zrow.shape[0] - 1
    grid_spec = pltpu.PrefetchScalarGridSpec(
        num_scalar_prefetch=1,
        grid=(T // tm,),
        in_specs=[pl.BlockSpec((tm * S, LANES), lambda i, z: (i, 0)), pl.BlockSpec(memory_space=pl.ANY)],
        out_specs=pl.BlockSpec(memory_space=pl.ANY),
        scratch_shapes=[pltpu.SMEM((2 * tm,), jnp.int32), pltpu.VMEM((tme * S, LANES), xn.dtype),
                        pltpu.SemaphoreType.DMA, pltpu.SemaphoreType.DMA, pltpu.SemaphoreType.DMA],
    )
    return pl.pallas_call(
        functools.partial(_moe_dispatch_kernel, E=E, tm=tm, tme=tme, S=S),
        grid_spec=grid_spec,
        out_shape=jax.ShapeDtypeStruct((P * S, LANES), xn.dtype),
        compiler_params=_cparams("arbitrary"),
        name="moe_dispatch",
    )(zrow, xn, dest)


def _moe_ffn_kernel(te_ref, nu_ref, x_ref, wg_ref, wu_ref, wd_ref, o_ref, xn_s, acc_s, *, S):
    i, f = pl.program_id(0), pl.program_id(1)
    used = i < nu_ref[0]
    tme, D = xn_s.shape

    @pl.when(used)
    def _():
        @pl.when(f == 0)
        def _():
            for s in range(S):
                xn_s[:, s * LANES:(s + 1) * LANES] = _slab_load(x_ref, s, tme, S).astype(BF16)
            acc_s[...] = jnp.zeros(acc_s.shape, F32)

        xn = xn_s[...]
        h = _silu(_dot(xn, wg_ref[...])) * _dot(xn, wu_ref[...])
        acc_s[...] += _dot(h.astype(BF16), wd_ref[...])

        @pl.when(f == pl.num_programs(1) - 1)
        def _():
            for s in range(S):
                _slab_store(o_ref, s, tme, S, acc_s[:, s * LANES:(s + 1) * LANES])

    @pl.when(jnp.logical_not(used) & (f == 0))
    def _():
        o_ref[...] = jnp.zeros(o_ref.shape, F32)


def _moe_ffn(xs, tile_expert, n_used, wg, wu, wd, D, tme, tf):
    S = D // LANES
    P = xs.shape[0] // S
    nf = wg.shape[2] // tf
    feff = lambda i, f, nu: jnp.where(i < nu[0], f, nf - 1)
    grid_spec = pltpu.PrefetchScalarGridSpec(
        num_scalar_prefetch=2,
        grid=(P // tme, nf),
        in_specs=[pl.BlockSpec((tme * S, LANES), lambda i, f, te, nu: (jnp.minimum(i, nu[0] - 1), 0)),
                  pl.BlockSpec((None, D, tf), lambda i, f, te, nu: (te[i], 0, feff(i, f, nu))),
                  pl.BlockSpec((None, D, tf), lambda i, f, te, nu: (te[i], 0, feff(i, f, nu))),
                  pl.BlockSpec((None, tf, D), lambda i, f, te, nu: (te[i], feff(i, f, nu), 0))],
        out_specs=pl.BlockSpec((tme * S, LANES), lambda i, f, te, nu: (i, 0)),
        scratch_shapes=[pltpu.VMEM((tme, D), BF16), pltpu.VMEM((tme, D), F32)],
    )
    return pl.pallas_call(
        functools.partial(_moe_ffn_kernel, S=S),
        grid_spec=grid_spec,
        out_shape=jax.ShapeDtypeStruct((P * S, LANES), F32),
        compiler_params=_cparams("arbitrary", "arbitrary"),
        name="moe_ffn",
    )(tile_expert, n_used, xs, wg, wu, wd)


def _moe_combine_kernel(x_ref, info_ref, dest_hbm, ys_hbm, o_ref, dsm, buf0, buf1, sem_d, sem_r, *, E, tm, S):
    i, n = pl.program_id(0), pl.num_programs(0)
    slot = i % 2
    bufs = (buf0, buf1)

    def row_copy(sl, t, k):
        src = ys_hbm.at[pl.ds(pl.multiple_of(dsm[sl * 2 * tm + 2 * t + k] * S, S), S)]
        return pltpu.make_async_copy(src, bufs[k].at[sl, pl.ds(pl.multiple_of(t * S, S), S)], sem_r.at[sl])

    def fetch(tile, sl):
        dcopy = pltpu.make_async_copy(dest_hbm.at[pl.ds(pl.multiple_of(tile * 2 * tm, 2 * tm), 2 * tm)],
                                      dsm.at[pl.ds(pl.multiple_of(sl * 2 * tm, 2 * tm), 2 * tm)], sem_d)
        dcopy.start()
        dcopy.wait()

        def issue(t, carry):
            row_copy(sl, t, 0).start()
            row_copy(sl, t, 1).start()
            return carry

        lax.fori_loop(0, tm, issue, 0, unroll=8)

    @pl.when(i == 0)
    def _():
        fetch(i, slot)

    @pl.when(i + 1 < n)
    def _():
        fetch(i + 1, 1 - slot)

    def drain(t, carry):
        row_copy(slot, t, 0).wait()
        row_copy(slot, t, 1).wait()
        return carry

    lax.fori_loop(0, tm, drain, 0, unroll=8)
    w1 = info_ref[:, E + 2:E + 3]
    w2 = info_ref[:, E + 3:E + 4]
    y0, y1 = buf0.at[slot], buf1.at[slot]
    for s in range(S):
        cs = slice(s * LANES, (s + 1) * LANES)
        o_ref[:, cs] = x_ref[:, cs] + w1 * _slab_load(y0, s, tm, S) + w2 * _slab_load(y1, s, tm, S)


def _moe_combine(x, info, dest, ys, E, tm):
    T, D = x.shape
    S = D // LANES
    return pl.pallas_call(
        functools.partial(_moe_combine_kernel, E=E, tm=tm, S=S),
        grid=(T // tm,),
        in_specs=[pl.BlockSpec((tm, D), lambda i: (i, 0)), pl.BlockSpec((tm, LANES), lambda i: (i, 0)),
                  pl.BlockSpec(memory_space=pl.ANY), pl.BlockSpec(memory_space=pl.ANY)],
        out_specs=pl.BlockSpec((tm, D), lambda i: (i, 0)),
        out_shape=jax.ShapeDtypeStruct((T, D), F32),
        scratch_shapes=[pltpu.SMEM((4 * tm,), jnp.int32), pltpu.VMEM((2, tm * S, LANES), F32),
                        pltpu.VMEM((2, tm * S, LANES), F32), pltpu.SemaphoreType.DMA,
                        pltpu.SemaphoreType.DMA((2,))],
        compiler_params=_cparams("arbitrary"),
        name="moe_combine",
    )(x, info, dest, ys)


def _moe_plan(info, E, tme, n_tiles):
    e_flat = info[:, E:E + 2].astype(jnp.int32).reshape(-1)
    onehot = (e_flat[:, None] == jnp.arange(E, dtype=jnp.int32)[None, :]).astype(jnp.int32)
    csum = jnp.cumsum(onehot, axis=0)
    counts = csum[-1]
    rank = jnp.take_along_axis(csum, e_flat[:, None], axis=1)[:, 0] - 1
    padded = ((counts + tme - 1) // tme) * tme
    ends = jnp.cumsum(padded)
    dest = (ends - padded)[e_flat] + rank
    n_used = (ends[-1] // tme).astype(jnp.int32)
    tile = jnp.arange(n_tiles, dtype=jnp.int32)
    tile_expert = jnp.sum((tile[:, None] * tme >= ends[None, :]).astype(jnp.int32), axis=1)
    tile_expert = jnp.minimum(tile_expert, E - 1)
    last_used = tile_expert[jnp.maximum(n_used - 1, 0)]
    tile_expert = jnp.where(tile < n_used, tile_expert, last_used)
    zrow = jnp.concatenate([jnp.where(counts > 0, ends - tme, -1), n_used.reshape(1)]).astype(jnp.int32)
    return dest.astype(jnp.int32), tile_expert.astype(jnp.int32), n_used.reshape(1), zrow


def _norm_kernel(x_ref, g_ref, o_ref):
    o_ref[...] = _rms(x_ref[...], g_ref[...])


def _final_norm(x, g, row0, n_rows, tm):
    D = x.shape[1]
    return pl.pallas_call(
        _norm_kernel,
        grid=(n_rows // tm,),
        in_specs=[pl.BlockSpec((tm, D), lambda i: (row0 // tm + i, 0)), _resident((1, D))],
        out_specs=pl.BlockSpec((tm, D), lambda i: (i, 0)),
        out_shape=jax.ShapeDtypeStruct((n_rows, D), F32),
        compiler_params=_cparams("parallel"),
        name="final_norm",
    )(x, g)


def _block_diag(w):
    n, c, d = w.shape
    eye = jnp.eye(n, dtype=w.dtype)
    return (w[:, :, None, :] * eye[:, None, :, None]).reshape(n * c, n * d)


def _pad_to(x, axis, mult):
    pad = (-x.shape[axis]) % mult
    if pad == 0:
        return x
    widths = [(0, 0)] * x.ndim
    widths[axis] = (0, pad)
    return jnp.pad(x, widths)


def kernel(x_prompt, x_sample, mem_prompt, cache_ckv, cache_krope, page_table, state_pool, state_conv, state_h, cache_mem_k, cache_mem_v, norm_mix, norm_mem, norm_memkv, norm_ffn, norm_final, w_in, w_pool, pool_scale, conv_w, conv_b, w_rg_a, b_rg_a, w_rg_x, b_rg_x, lru_lambda, q_norm, w_uq, kv_norm, w_uk, w_uv, w_out, w_mq, w_mk, w_mv, w_mo, ffn_w_gate, ffn_w_up, ffn_w_down, moe_router, moe_w_gate, moe_w_up, moe_w_down):
    B, L, D = x_prompt.shape
    BS, LS, _ = x_sample.shape
    depth = w_in.shape[0]
    d_pool = state_pool.shape[-1]
    d_rnn = state_conv.shape[-1]
    r_q = q_norm.shape[-1]
    r_kv = kv_norm.shape[-1]
    H = w_uq.shape[2]
    M, MH, MHD = cache_mem_k.shape[2:]
    E = moe_router.shape[-1]
    PS = cache_ckv.shape[2]
    n_pages = page_table.shape[1]
    past_len = n_pages * PS
    NP, NS = B * L, BS * LS
    T = NP + NS
    assert d_pool == len(POOL_WINDOWS) * LANES and MHD == LANES and w_uq.shape[3] == D_NOPE + D_ROPE

    tm = min(512, NS)
    assert NP % tm == 0 and NS % tm == 0
    tq = min(512, L)
    TL = min(512, L)
    G = max(1, min(16, n_pages // 2))
    assert n_pages % G == 0
    offs = (0, d_pool, d_pool + d_rnn, d_pool + 2 * d_rnn, d_pool + 2 * d_rnn + r_q,
            d_pool + 2 * d_rnn + r_q + r_kv)
    n_in = offs[-1] + D_ROPE
    offs = offs + (offs[-1] + LANES,)

    half = D_ROPE // 2
    freqs = ROPE_BASE ** (-jnp.arange(half, dtype=F32) / half)
    pos_p = jnp.arange(L, dtype=jnp.int32)
    pos_s = past_len + jnp.arange(LS, dtype=jnp.int32)

    def tables(pos):
        ang = pos.astype(F32)[:, None] * freqs[None, :]
        c, s = jnp.cos(ang), jnp.sin(ang)
        return jnp.concatenate([c, c], -1), jnp.concatenate([-s, s], -1)

    cos_p, sin_p = tables(pos_p)
    cos_s, sin_s = tables(pos_s)
    cos_all = jnp.concatenate([jnp.tile(cos_p, (B, 1)), jnp.tile(cos_s, (BS, 1))], 0)
    sin_all = jnp.concatenate([jnp.tile(sin_p, (B, 1)), jnp.tile(sin_s, (BS, 1))], 0)

    x = jnp.concatenate([x_prompt.reshape(NP, D), x_sample.reshape(NS, D)], 0)
    mem2 = mem_prompt.reshape(B * M, D)
    krope_t = jnp.swapaxes(cache_krope, 2, 3)
    cmk = cache_mem_k.reshape(depth, BS, M * MH, MHD)
    cmv = cache_mem_v.reshape(depth, BS, M * MH, MHD)
    row = lambda v: v.reshape(1, -1)

    outs = {k: [] for k in ('pc', 'pk', 'pp', 'pcv', 'ph', 'pmk', 'pmv', 'sc', 'sk', 'sp', 'scv', 'sh')}
    for l in range(depth):
        w_in_p = _pad_to(w_in[l], 1, LANES)
        w_in_p = jnp.pad(w_in_p, ((0, 0), (0, offs[-1] - w_in_p.shape[1]))).astype(BF16)
        wuq = w_uq[l]
        w_uq_p = jnp.concatenate([wuq[:, :, :D_NOPE].reshape(r_q, H * D_NOPE),
                                  wuq[:, :, D_NOPE:].reshape(r_q, H * D_ROPE)], 1).astype(BF16)
        wuk_t = jnp.transpose(w_uk[l], (1, 2, 0)).astype(BF16)
        wuv_h = jnp.transpose(w_uv[l], (1, 0, 2)).astype(BF16)
        lw = {
            'w_pool': w_pool[l].astype(BF16), 'pool_scale': row(pool_scale[l]),
            'conv_w': conv_w[l], 'conv_b': row(conv_b[l]),
            'w_rg_a_bd': _block_diag(w_rg_a[l]).astype(BF16), 'b_rg_a': row(b_rg_a[l]),
            'w_rg_x_bd': _block_diag(w_rg_x[l]).astype(BF16), 'b_rg_x': row(b_rg_x[l]),
            'lru_lambda': row(lru_lambda[l]),
        }

        u, xb, gb, q, ckv, kr = _in_proj(x, row(norm_mix[l]), w_in_p, row(q_norm[l]), w_uq_p,
                                         row(kv_norm[l]), cos_all, sin_all, offs, tm)

        y_pool, y_rnn, pool_new, conv_new, h_new = _seqmix_prompt(u, xb, gb, lw, B, L, TL)
        tmaj = lambda a: jnp.transpose(a[NP:].reshape(BS, LS, -1), (1, 0, 2))
        u_s, xb_s, gb_s = tmaj(u), tmaj(xb), tmaj(gb)
        yp_s, yr_s, h_s = _seqmix_sample(u_s, xb_s, gb_s, jnp.transpose(state_pool[l], (1, 0, 2)),
                                         jnp.transpose(state_conv[l], (1, 0, 2)), state_h[l], lw, past_len)
        bmaj = lambda a: jnp.transpose(a, (1, 0, 2)).reshape(NS, -1)

        y_mla = _attn_prompt(q, cos_all, sin_all, ckv, kr, wuk_t, wuv_h, B, L, tq)
        s_ckv = ckv[NP:].reshape(BS, LS, r_kv)
        s_kr = kr[NP:].reshape(BS, LS, D_ROPE)
        y_mla_s = _attn_sample(page_table, q[NP:].reshape(BS, LS, -1), cos_s, sin_s, s_ckv, s_kr,
                               wuk_t, wuv_h, cache_ckv, krope_t, l, G)
        x = _mm_res([(y_pool, bmaj(yp_s)), (y_rnn, bmaj(yr_s)), (y_mla, y_mla_s.reshape(NS, -1))],
                    w_out[l].astype(BF16), x, tm, "out_proj")

        w_mkv = jnp.concatenate([w_mk[l].reshape(D, MH * MHD), w_mv[l].reshape(D, MH * MHD)], 1).astype(BF16)
        mk_p, mv_p = _norm_mm(mem2, row(norm_memkv[l]), w_mkv,
                              ((0, MH * MHD), (MH * MHD, 2 * MH * MHD)), min(512, B * M), "mem_kv")
        (qm,) = _norm_mm(x, row(norm_mem[l]), w_mq[l].reshape(D, MH * MHD).astype(BF16),
                         ((0, MH * MHD),), tm, "mem_q")
        lq_p = min(512, L)
        o_mem = _mem_attn(qm, mk_p.reshape(B, M, -1), mv_p.reshape(B, M, -1), 0, NP // lq_p, lq_p, 1,
                          lambda i: (i // (L // lq_p), 0, 0), NP, False, "mem_attn_prompt")
        nb_s = 8 if BS % 8 == 0 else 1
        o_mem_s = _mem_attn(qm, cmk, cmv, NP, BS, LS, nb_s, lambda i: (l, i, 0, 0), NS, True,
                            "mem_attn_sample")
        x = _mm_res([(o_mem, o_mem_s)], w_mo[l].reshape(MH * MHD, D).astype(BF16), x, tm, "mem_out")

        j = l // 2
        if l % 2 == 0:
            x = _ffn(x, row(norm_ffn[l]), ffn_w_gate[j].astype(BF16), ffn_w_up[j].astype(BF16),
                     ffn_w_down[j].astype(BF16), tm, 512)
        else:
            wr_p = _pad_to(moe_router[j], 1, LANES)
            xn, info = _router(x, row(norm_ffn[l]), wr_p, E, tm)
            tf = 512
            wg = _pad_to(moe_w_gate[j], 2, tf).astype(BF16)
            wu = _pad_to(moe_w_up[j], 2, tf).astype(BF16)
            wd = _pad_to(moe_w_down[j], 1, tf).astype(BF16)
            tme = 512 if T >= 4096 else 128
            n_tiles = (2 * T + E * (tme - 1) + tme - 1) // tme
            dest, tile_expert, n_used, zrow = _moe_plan(info, E, tme, n_tiles)
            xs = _moe_dispatch(xn, dest, zrow, n_tiles * tme, D, tm, tme)
            ys = _moe_ffn(xs, tile_expert, n_used, wg, wu, wd, D, tme, tf)
            x = _moe_combine(x, info, dest, ys, E, tm)

        outs['pc'].append(ckv[:NP].reshape(B, L, r_kv))
        outs['pk'].append(kr[:NP].reshape(B, L, D_ROPE))
        outs['pp'].append(pool_new)
        outs['pcv'].append(conv_new)
        outs['ph'].append(h_new.reshape(B, d_rnn))
        outs['pmk'].append(mk_p.reshape(B, M, MH, MHD))
        outs['pmv'].append(mv_p.reshape(B, M, MH, MHD))
        outs['sc'].append(s_ckv)
        outs['sk'].append(s_kr)
        u_sb = u[NP:].reshape(BS, LS, d_pool)
        xb_sb = xb[NP:].reshape(BS, LS, d_rnn)
        outs['sp'].append(jnp.concatenate([state_pool[l], u_sb], 1)[:, -POOL_STATE:])
        outs['scv'].append(jnp.concatenate([state_conv[l], xb_sb], 1)[:, -(CONV_W - 1):])
        outs['sh'].append(h_s)

    y_prompt = _final_norm(x, row(norm_final), 0, NP, tm).reshape(B, L, D)
    y_sample = _final_norm(x, row(norm_final), NP, NS, tm).reshape(BS, LS, D)
    st = lambda k: jnp.stack(outs[k])
    return (y_prompt, y_sample, st('pc'), st('pk'), st('pp'), st('pcv'), st('ph'), st('pmk'), st('pmv'),
            st('sc'), st('sk'), st('sp'), st('scv'), st('sh'))
```
